```python
import math
import jax, jax.numpy as jnp
from jax import lax
import numpy as np

D_MODEL = 4096
BATCH = 1
SEQ = 8192
DEPTH = 1
DEC_BATCH = 32
DEC_SEQ = 4
PAST_LEN = 8192
PAGE_SIZE = 128

HEAD_DIM = 128
ATT_HEADS = D_MODEL // (2 * HEAD_DIM)
ATT_WIDTH = ATT_HEADS * HEAD_DIM
CONV_CH = D_MODEL - ATT_WIDTH
CONV_W = 31
DILATED_BRANCHES = ((128, 1), (512, 4), (2048, 16))
W_MAX = 2048
Q_BLOCK = 128
MEM_TOKENS = 256
MEM_HEADS = 4
MEM_HEAD_DIM = D_MODEL // MEM_HEADS
N_GROUPS = 8
EXPERTS_PER_GROUP = 8
N_EXPERTS = N_GROUPS * EXPERTS_PER_GROUP
TOP_K_INNER = 2
EXPERT_FF = D_MODEL // 4
EXPERT_BLOCK = 128
ALPHA = (2 * DEPTH) ** 0.25
BETA = (8 * DEPTH) ** -0.25
LN_EPS = 1e-5

kernel_name = 'hybrid_dilated_conv_hmoe_step'


def layer_norm(x, g, b):
    xf = x.astype(jnp.float32)
    mu = jnp.mean(xf, axis=-1, keepdims=True)
    var = jnp.mean(jnp.square(xf - mu), axis=-1, keepdims=True)
    return ((xf - mu) * lax.rsqrt(var + LN_EPS) * g + b).astype(x.dtype)


def dilated_window_attention(q, k_buf, v_buf, off):
    b, t, h, hd = q.shape
    qb = Q_BLOCK if t % Q_BLOCK == 0 else t
    n_blk = t // qb
    scale = 1.0 / math.sqrt(hd)

    def block(i0):
        q_blk = lax.dynamic_slice_in_dim(q, i0, qb, axis=1).astype(jnp.float32) * scale
        outs, lses = [], []
        for (w, d) in DILATED_BRANCHES:
            n_keys = w // d + 1
            rows = off + i0 + jnp.arange(qb)[:, None] - d * jnp.arange(n_keys)[None, :]
            valid = rows >= 0
            rows = jnp.maximum(rows, 0)
            k_g = k_buf[:, rows].astype(jnp.float32)
            v_g = v_buf[:, rows].astype(jnp.float32)
            s = jnp.einsum('bqhd,bqnhd->bhqn', q_blk, k_g)
            s = jnp.where(valid[None, None], s, -jnp.inf)
            lse = jax.nn.logsumexp(s, axis=-1)
            p = jnp.exp(s - lse[..., None])
            outs.append(jnp.einsum('bhqn,bqnhd->bqhd', p, v_g))
            lses.append(lse)
        wts = jax.nn.softmax(jnp.stack(lses, axis=0), axis=0)
        out = jnp.einsum('rbhq,rbqhd->bqhd', wts, jnp.stack(outs, axis=0))
        return out.astype(q.dtype)

    blocks = lax.map(block, jnp.arange(n_blk) * qb)
    return jnp.transpose(blocks, (1, 0, 2, 3, 4)).reshape(b, t, h, hd)


def causal_depthwise_conv(u_buf, w):
    return lax.conv_general_dilated(u_buf, w[:, None, :].astype(u_buf.dtype), window_strides=(1,),
                                    padding='VALID', dimension_numbers=('NWC', 'WIO', 'NWC'),
                                    feature_group_count=u_buf.shape[-1])


def memory_attention(h, mem_k, mem_v, w_q, w_o):
    b, t, _ = h.shape
    q = (h @ w_q).reshape(b, t, MEM_HEADS, MEM_HEAD_DIM)
    s = jnp.einsum('bthd,bmhd->bhtm', q, mem_k).astype(jnp.float32) / math.sqrt(MEM_HEAD_DIM)
    p = jax.nn.softmax(s, axis=-1)
    o = jnp.einsum('bhtm,bmhd->bthd', p, mem_v.astype(jnp.float32)).astype(h.dtype)
    return o.reshape(b, t, D_MODEL) @ w_o


def sorted_expert_ffn(x, eid, gate, w_gate, w_up, w_down):
    n, d = x.shape
    k = eid.shape[1]
    flat_e = eid.reshape(-1)
    flat_tok = jnp.repeat(jnp.arange(n, dtype=jnp.int32), k)
    flat_g = gate.reshape(-1)
    order = jnp.argsort(flat_e)
    e_s, tok_s, g_s = flat_e[order], flat_tok[order], flat_g[order]
    counts = jnp.bincount(flat_e, length=N_EXPERTS)
    start = jnp.cumsum(counts) - counts
    padded = (counts + EXPERT_BLOCK - 1) // EXPERT_BLOCK * EXPERT_BLOCK
    p_end = jnp.cumsum(padded)
    p_start = p_end - padded
    dest = p_start[e_s] + (jnp.arange(n * k) - start[e_s])
    n_blocks = -(-(n * k + N_EXPERTS * (EXPERT_BLOCK - 1)) // EXPERT_BLOCK)
    n_rows = n_blocks * EXPERT_BLOCK
    row_tok = jnp.full((n_rows,), n, jnp.int32).at[dest].set(tok_s)
    row_g = jnp.zeros((n_rows,), x.dtype).at[dest].set(g_s.astype(x.dtype))
    blk_e = jnp.minimum(jnp.searchsorted(p_end, jnp.arange(n_blocks) * EXPERT_BLOCK, side='right'),
                        N_EXPERTS - 1)
    x_pad = jnp.concatenate([x, jnp.zeros((1, d), x.dtype)], axis=0)
    xs = x_pad[row_tok].reshape(n_blocks, EXPERT_BLOCK, d)

    def run(args):
        xb, e = args
        hb = jax.nn.silu(xb @ w_gate[e]) * (xb @ w_up[e])
        return hb @ w_down[e]

    ys = lax.map(run, (xs, blk_e)).reshape(n_rows, d) * row_g[:, None]
    return jax.ops.segment_sum(ys, row_tok, num_segments=n + 1)[:n]


def hierarchical_moe(x, w_rg, b_rg, w_re, b_re, w_gate, w_up, w_down):
    xf = x.astype(jnp.float32)
    g_logits = xf @ w_rg.astype(jnp.float32) + b_rg.astype(jnp.float32)
    grp = jnp.argmax(g_logits, axis=-1)
    p_grp = jnp.take_along_axis(jax.nn.softmax(g_logits, axis=-1), grp[:, None], axis=1)
    e_logits = jnp.einsum('nd,dge->nge', xf, w_re.astype(jnp.float32)) + b_re.astype(jnp.float32)
    e_logits = jnp.take_along_axis(e_logits, grp[:, None, None], axis=1)[:, 0]
    top_v, top_i = lax.top_k(e_logits, TOP_K_INNER)
    gate = p_grp * jax.nn.softmax(top_v, axis=-1)
    eid = (grp[:, None] * EXPERTS_PER_GROUP + top_i).astype(jnp.int32)
    return sorted_expert_ffn(x, eid, gate, w_gate, w_up, w_down)


def trunk_layer(x, k_prev, v_prev, conv_prev, mem_k, mem_v,
                w_in, conv_w, conv_b, conv_ln_g, conv_ln_b, w_out, ln1_g, ln1_b,
                w_mem_q, w_mem_o, ln2_g, ln2_b,
                w_router_group, b_router_group, w_router_expert, b_router_expert,
                w_exp_gate, w_exp_up, w_exp_down, ln3_g, ln3_b):
    b, t, _ = x.shape
    proj = x @ w_in
    q, k, v, g_a, g_b = jnp.split(
        proj, [ATT_WIDTH, 2 * ATT_WIDTH, 3 * ATT_WIDTH, 3 * ATT_WIDTH + CONV_CH], axis=-1)
    k_buf = jnp.concatenate([k_prev, k.reshape(b, t, ATT_HEADS, HEAD_DIM)], axis=1)
    v_buf = jnp.concatenate([v_prev, v.reshape(b, t, ATT_HEADS, HEAD_DIM)], axis=1)
    attn = dilated_window_attention(q.reshape(b, t, ATT_HEADS, HEAD_DIM), k_buf, v_buf,
                                    k_prev.shape[1]).reshape(b, t, ATT_WIDTH)
    u_buf = jnp.concatenate([conv_prev, g_a * jax.nn.sigmoid(g_b)], axis=1)
    c = jax.nn.silu(layer_norm(causal_depthwise_conv(u_buf, conv_w) + conv_b, conv_ln_g, conv_ln_b))
    h1 = layer_norm(ALPHA * x + jnp.concatenate([attn, c], axis=-1) @ w_out, ln1_g, ln1_b)
    h2 = layer_norm(ALPHA * h1 + memory_attention(h1, mem_k, mem_v, w_mem_q, w_mem_o), ln2_g, ln2_b)
    ffn = hierarchical_moe(h2.reshape(b * t, D_MODEL), w_router_group, b_router_group,
                           w_router_expert, b_router_expert, w_exp_gate, w_exp_up,
                           w_exp_down).reshape(b, t, D_MODEL)
    y = layer_norm(ALPHA * h2 + ffn, ln3_g, ln3_b)
    keep = min(W_MAX, k_buf.shape[1])
    return y, k_buf[:, -keep:], v_buf[:, -keep:], u_buf[:, -(CONV_W - 1):]


def setup_inputs(seed: int = 0) -> dict:
    key = jax.random.key(seed)
    ks = iter(jax.random.split(key, 40))
    nrm = lambda shape, scale: jax.random.normal(next(ks), shape, jnp.float32) * scale
    win_len = min(W_MAX, PAST_LEN)
    n_in = 3 * ATT_WIDTH + 2 * CONV_CH
    di = D_MODEL ** -0.5
    return {
        'x_prompt': nrm((BATCH, SEQ, D_MODEL), 1.0),
        'x_sample': nrm((DEC_BATCH, DEC_SEQ, D_MODEL), 1.0),
        'mem_prompt': nrm((BATCH, MEM_TOKENS, D_MODEL), 1.0),
        'cache_win_k': nrm((DEPTH, DEC_BATCH, win_len, ATT_HEADS, HEAD_DIM), 1.0),
        'cache_win_v': nrm((DEPTH, DEC_BATCH, win_len, ATT_HEADS, HEAD_DIM), 1.0),
        'state_conv': nrm((DEPTH, DEC_BATCH, CONV_W - 1, CONV_CH), 0.5),
        'cache_mem_k': nrm((DEPTH, DEC_BATCH, MEM_TOKENS, MEM_HEADS, MEM_HEAD_DIM), 1.0),
        'cache_mem_v': nrm((DEPTH, DEC_BATCH, MEM_TOKENS, MEM_HEADS, MEM_HEAD_DIM), 1.0),
        'w_in': nrm((DEPTH, D_MODEL, n_in), di),
        'conv_w': nrm((DEPTH, CONV_W, CONV_CH), CONV_W ** -0.5),
        'conv_b': nrm((DEPTH, CONV_CH), 0.01),
        'conv_ln_g': 1.0 + nrm((DEPTH, CONV_CH), 0.01),
        'conv_ln_b': nrm((DEPTH, CONV_CH), 0.01),
        'w_out': nrm((DEPTH, D_MODEL, D_MODEL), di * BETA),
        'ln1_g': 1.0 + nrm((DEPTH, D_MODEL), 0.01),
        'ln1_b': nrm((DEPTH, D_MODEL), 0.01),
        'w_mem_q': nrm((DEPTH, D_MODEL, D_MODEL), di),
        'w_mem_k': nrm((DEPTH, D_MODEL, D_MODEL), di),
        'w_mem_v': nrm((DEPTH, D_MODEL, D_MODEL), di),
        'w_mem_o': nrm((DEPTH, D_MODEL, D_MODEL), di * BETA),
        'ln2_g': 1.0 + nrm((DEPTH, D_MODEL), 0.01),
        'ln2_b': nrm((DEPTH, D_MODEL), 0.01),
        'w_router_group': nrm((DEPTH, D_MODEL, N_GROUPS), di),
        'b_router_group': nrm((DEPTH, N_GROUPS), 0.01),
        'w_router_expert': nrm((DEPTH, D_MODEL, N_GROUPS, EXPERTS_PER_GROUP), di),
        'b_router_expert': nrm((DEPTH, N_GROUPS, EXPERTS_PER_GROUP), 0.01),
        'w_exp_gate': nrm((DEPTH, N_EXPERTS, D_MODEL, EXPERT_FF), di),
        'w_exp_up': nrm((DEPTH, N_EXPERTS, D_MODEL, EXPERT_FF), di),
        'w_exp_down': nrm((DEPTH, N_EXPERTS, EXPERT_FF, D_MODEL), EXPERT_FF ** -0.5 * BETA),
        'ln3_g': 1.0 + nrm((DEPTH, D_MODEL), 0.01),
        'ln3_b': nrm((DEPTH, D_MODEL), 0.01),
    }


def reference(x_prompt, x_sample, mem_prompt, cache_win_k, cache_win_v, state_conv,
              cache_mem_k, cache_mem_v, w_in, conv_w, conv_b, conv_ln_g, conv_ln_b, w_out,
              ln1_g, ln1_b, w_mem_q, w_mem_k, w_mem_v, w_mem_o, ln2_g, ln2_b,
              w_router_group, b_router_group, w_router_expert, b_router_expert,
              w_exp_gate, w_exp_up, w_exp_down, ln3_g, ln3_b):
    bp = x_prompt.shape[0]
    hp, hs = x_prompt, x_sample
    wkp, wvp, cvp, mkp, mvp, wks, wvs, cvs = [], [], [], [], [], [], [], []
    for l in range(DEPTH):
        lw = (w_in[l], conv_w[l], conv_b[l], conv_ln_g[l], conv_ln_b[l], w_out[l], ln1_g[l], ln1_b[l],
              w_mem_q[l], w_mem_o[l], ln2_g[l], ln2_b[l], w_router_group[l], b_router_group[l],
              w_router_expert[l], b_router_expert[l], w_exp_gate[l], w_exp_up[l], w_exp_down[l],
              ln3_g[l], ln3_b[l])
        mk = (mem_prompt @ w_mem_k[l]).reshape(bp, MEM_TOKENS, MEM_HEADS, MEM_HEAD_DIM)
        mv = (mem_prompt @ w_mem_v[l]).reshape(bp, MEM_TOKENS, MEM_HEADS, MEM_HEAD_DIM)
        kv0 = jnp.zeros((bp, 0, ATT_HEADS, HEAD_DIM), x_prompt.dtype)
        cv0 = jnp.zeros((bp, CONV_W - 1, CONV_CH), x_prompt.dtype)
        hp, kp, vp, cp = trunk_layer(hp, kv0, kv0, cv0, mk, mv, *lw)
        hs, ks_, vs_, cs_ = trunk_layer(hs, cache_win_k[l], cache_win_v[l], state_conv[l],
                                        cache_mem_k[l], cache_mem_v[l], *lw)
        wkp.append(kp); wvp.append(vp); cvp.append(cp); mkp.append(mk); mvp.append(mv)
        wks.append(ks_); wvs.append(vs_); cvs.append(cs_)
    return (hp, hs, jnp.stack(wkp), jnp.stack(wvp), jnp.stack(cvp), jnp.stack(mkp), jnp.stack(mvp),
            jnp.stack(wks), jnp.stack(wvs), jnp.stack(cvs))
```

```python
import functools
import math

import jax
import jax.numpy as jnp
from jax import lax
from jax.experimental import pallas as pl
from jax.experimental.pallas import tpu as pltpu

F32 = jnp.float32
BF16 = jnp.bfloat16

LANES = 128
HEAD_DIM = 128
DILATED_BRANCHES = ((128, 1), (512, 4), (2048, 16))
W_MAX = 2048
KEYS_PER_BLOCK = 128
CONV_W = 31
CONV_HALO = 32
EXPERT_BLOCK = 128
LN_EPS = 1e-5
VMEM_LIMIT = 56 * 1024 * 1024


def _params(*sem):
    return pltpu.CompilerParams(dimension_semantics=sem, vmem_limit_bytes=VMEM_LIMIT)


def _pick(n, candidates):
    for c in candidates:
        if n % c == 0:
            return c
    return n


def _mm_kernel(a_ref, b_ref, o_ref):
    o_ref[...] = jnp.dot(a_ref[...].astype(BF16), b_ref[...].astype(BF16),
                         preferred_element_type=F32).astype(o_ref.dtype)


def _matmul(a, b, out_dtype=F32):
    m, k = a.shape
    n = b.shape[1]
    tm = _pick(m, (640, 512, 256, 128))
    tn = _pick(n, (512, 256, 128))
    return pl.pallas_call(
        _mm_kernel,
        grid=(m // tm, n // tn),
        in_specs=[pl.BlockSpec((tm, k), lambda i, j: (i, 0)),
                  pl.BlockSpec((k, tn), lambda i, j: (0, j))],
        out_specs=pl.BlockSpec((tm, tn), lambda i, j: (i, j)),
        out_shape=jax.ShapeDtypeStruct((m, n), out_dtype),
        compiler_params=_params("parallel", "arbitrary"),
        name="matmul",
    )(a, b)


def _mm_slab_kernel(a_ref, b_ref, o_ref):
    r = jnp.dot(a_ref[...].astype(BF16), b_ref[...].astype(BF16), preferred_element_type=F32)
    for s in range(o_ref.shape[0]):
        o_ref[s] = r[:, s * LANES:(s + 1) * LANES]


def _matmul_slabs(a, b):
    m, k = a.shape
    n = b.shape[1]
    tm = _pick(m, (640, 512, 256, 128))
    tn = _pick(n, (512, 256, 128))
    return pl.pallas_call(
        _mm_slab_kernel,
        grid=(m // tm, n // tn),
        in_specs=[pl.BlockSpec((tm, k), lambda i, j: (i, 0)),
                  pl.BlockSpec((k, tn), lambda i, j: (0, j))],
        out_specs=pl.BlockSpec((tn // LANES, tm, LANES), lambda i, j: (j, i, 0)),
        out_shape=jax.ShapeDtypeStruct((n // LANES, m, LANES), F32),
        compiler_params=_params("parallel", "arbitrary"),
        name="matmul_slabs",
    )(a, b)


def _ln_rows(x, g, b):
    mu = jnp.mean(x, axis=-1, keepdims=True)
    xc = x - mu
    var = jnp.mean(xc * xc, axis=-1, keepdims=True)
    return xc * lax.rsqrt(var + LN_EPS) * g + b


def _silu(x):
    return x * jax.nn.sigmoid(x)


def _res_ln_kernel(res_ref, t_ref, g_ref, b_ref, o_ref, obf_ref, *, alpha):
    y = _ln_rows(alpha * res_ref[...] + t_ref[...], g_ref[...], b_ref[...])
    o_ref[...] = y
    obf_ref[...] = y.astype(BF16)


def _res_ln(res, t, g, b, alpha):
    n, d = res.shape
    tm = _pick(n, (128, 64, 8))
    row = pl.BlockSpec((tm, d), lambda i: (i, 0))
    vec = pl.BlockSpec((1, d), lambda i: (0, 0))
    return pl.pallas_call(
        functools.partial(_res_ln_kernel, alpha=alpha),
        grid=(n // tm,),
        in_specs=[row, row, vec, vec],
        out_specs=[row, row],
        out_shape=[jax.ShapeDtypeStruct((n, d), F32), jax.ShapeDtypeStruct((n, d), BF16)],
        compiler_params=_params("parallel"),
        name="res_ln",
    )(res, t, g.reshape(1, d), b.reshape(1, d))


def _ln_silu_kernel(x_ref, g_ref, b_ref, o_ref):
    o_ref[...] = _silu(_ln_rows(x_ref[...], g_ref[...], b_ref[...])).astype(o_ref.dtype)


def _ln_silu(x, g, b):
    n, d = x.shape
    tm = _pick(n, (256, 128, 64, 8))
    row = pl.BlockSpec((tm, d), lambda i: (i, 0))
    vec = pl.BlockSpec((1, d), lambda i: (0, 0))
    return pl.pallas_call(
        _ln_silu_kernel,
        grid=(n // tm,),
        in_specs=[row, vec, vec],
        out_specs=row,
        out_shape=jax.ShapeDtypeStruct((n, d), BF16),
        compiler_params=_params("parallel"),
        name="ln_silu",
    )(x, g.reshape(1, d), b.reshape(1, d))


def _attn_prompt_kernel(q_ref, kp_ref, kc_ref, vp_ref, vc_ref, o_ref, acc_ref, m_ref, l_ref):
    c = pl.program_id(0)
    scale = 1.0 / math.sqrt(HEAD_DIM)
    blk = KEYS_PER_BLOCK
    ii = lax.broadcasted_iota(jnp.int32, (blk, blk), 0)
    jj = lax.broadcasted_iota(jnp.int32, (blk, blk), 1)
    nt = (((1,), (1,)), ((), ()))

    def rows(start, d):
        return pl.ds(start, blk) if d == 1 else pl.ds(start, blk, stride=d)

    def sub_block(r, b, d, first, last):
        cur = rows(r + d * blk * b, d)
        if b == 0:
            prev = rows(r + W_MAX - d * blk, d)
            k_a, v_a = kp_ref[prev, :], vp_ref[prev, :]
        else:
            prev = rows(r + d * blk * (b - 1), d)
            k_a, v_a = kc_ref[prev, :], vc_ref[prev, :]
        q = (q_ref[cur, :] * scale).astype(BF16)
        k_b, v_b = kc_ref[cur, :], vc_ref[cur, :]
        s_a = lax.dot_general(q, k_a.astype(BF16), nt, preferred_element_type=F32)
        s_b = lax.dot_general(q, k_b.astype(BF16), nt, preferred_element_type=F32)
        mask_a = jj >= ii
        if b == 0:
            mask_a = jnp.logical_and(mask_a, c > 0)
        s_a = jnp.where(mask_a, s_a, -jnp.inf)
        s_b = jnp.where(jj <= ii, s_b, -jnp.inf)
        m_loc = jnp.maximum(jnp.max(s_a, axis=1, keepdims=True), jnp.max(s_b, axis=1, keepdims=True))
        p_a = jnp.exp(s_a - m_loc)
        p_b = jnp.exp(s_b - m_loc)
        l_loc = jnp.sum(p_a, axis=1, keepdims=True) + jnp.sum(p_b, axis=1, keepdims=True)
        acc_loc = (jnp.dot(p_a.astype(BF16), v_a.astype(BF16), preferred_element_type=F32)
                   + jnp.dot(p_b.astype(BF16), v_b.astype(BF16), preferred_element_type=F32))
        if first:
            m_new, l_new, acc_new = m_loc, l_loc, acc_loc
        else:
            m_old = m_ref[cur, :][:, :1]
            l_old = l_ref[cur, :][:, :1]
            m_new = jnp.maximum(m_old, m_loc)
            a_old = jnp.exp(m_old - m_new)
            a_loc = jnp.exp(m_loc - m_new)
            l_new = a_old * l_old + a_loc * l_loc
            acc_new = a_old * acc_ref[cur, :] + a_loc * acc_loc
        if last:
            o_ref[cur, :] = (acc_new / l_new).astype(o_ref.dtype)
        else:
            m_ref[cur, :] = jnp.broadcast_to(m_new, (blk, LANES))
            l_ref[cur, :] = jnp.broadcast_to(l_new, (blk, LANES))
            acc_ref[cur, :] = acc_new

    order = sorted(DILATED_BRANCHES, key=lambda wd: -wd[1])
    for idx, (w, d) in enumerate(order):
        first, last = idx == 0, idx == len(order) - 1
        for b in range(W_MAX // (d * blk)):
            if d == 1:
                sub_block(0, b, d, first, last)
            else:
                def body(r, carry, b=b, d=d, first=first, last=last):
                    sub_block(r, b, d, first, last)
                    return carry
                lax.fori_loop(0, d, body, 0)


def _attn_prompt(proj3, t, heads):
    n_chunks = t // W_MAX
    blk = (None, W_MAX, LANES)
    cur = lambda off: pl.BlockSpec(blk, lambda c, h: (off + h, c, 0))
    prev = lambda off: pl.BlockSpec(blk, lambda c, h: (off + h, jnp.maximum(c - 1, 0), 0))
    return pl.pallas_call(
        _attn_prompt_kernel,
        grid=(n_chunks, heads),
        in_specs=[cur(0), prev(heads), cur(heads), prev(2 * heads), cur(2 * heads)],
        out_specs=pl.BlockSpec((W_MAX, LANES), lambda c, h: (c, h)),
        out_shape=jax.ShapeDtypeStruct((t, heads * LANES), BF16),
        scratch_shapes=[pltpu.VMEM((W_MAX, LANES), F32)] * 3,
        compiler_params=_params("parallel", "arbitrary"),
        name="attn_prompt",
    )(proj3, proj3, proj3, proj3, proj3)


SAMPLE_Q_PAD = 8
SAMPLE_K_PAD = 16


def _attn_sample_kernel(q_ref, kn_ref, vn_ref, kc_ref, vc_ref, o_ref, kbuf_ref, vbuf_ref, *, heads_per_step):
    past = kc_ref.shape[0]
    n_keys = past + SAMPLE_K_PAD
    scale = 1.0 / math.sqrt(HEAD_DIM)
    kbuf_ref[0:past, :] = kc_ref[...].astype(BF16)
    kbuf_ref[past:n_keys, :] = kn_ref[...].astype(BF16)
    vbuf_ref[0:past, :] = vc_ref[...].astype(BF16)
    vbuf_ref[past:n_keys, :] = vn_ref[...].astype(BF16)
    qi = lax.broadcasted_iota(jnp.int32, (SAMPLE_Q_PAD, n_keys), 0)
    rho = lax.broadcasted_iota(jnp.int32, (SAMPLE_Q_PAD, n_keys), 1)
    dist = past + qi - rho
    mult = jnp.zeros((SAMPLE_Q_PAD, n_keys), F32)
    for (w, d) in DILATED_BRANCHES:
        hit = jnp.logical_and(jnp.bitwise_and(dist, d - 1) == 0, dist <= w)
        mult = mult + jnp.where(hit, 1.0, 0.0)
    mult = jnp.where(dist >= 0, mult, 0.0)
    nt = (((1,), (1,)), ((), ()))
    for h in range(heads_per_step):
        sl = slice(h * HEAD_DIM, (h + 1) * HEAD_DIM)
        q = (q_ref[:, sl] * scale).astype(BF16)
        s = lax.dot_general(q, kbuf_ref[:, sl], nt, preferred_element_type=F32)
        s = jnp.where(mult > 0, s, -jnp.inf)
        m = jnp.max(s, axis=1, keepdims=True)
        p = mult * jnp.exp(s - m)
        l = jnp.sum(p, axis=1, keepdims=True)
        o = jnp.dot(p.astype(BF16), vbuf_ref[:, sl], preferred_element_type=F32)
        o_ref[:, sl] = (o / l).astype(o_ref.dtype)


def _attn_sample(q, k_new, v_new, k_cache, v_cache):
    bsz, past, width = k_cache.shape
    hps = 4 if (width // HEAD_DIM) % 4 == 0 else 1
    wg = hps * HEAD_DIM
    spec = lambda rows: pl.BlockSpec((None, rows, wg), lambda b, g: (b, 0, g))
    return pl.pallas_call(
        functools.partial(_attn_sample_kernel, heads_per_step=hps),
        grid=(bsz, width // wg),
        in_specs=[spec(SAMPLE_Q_PAD), spec(SAMPLE_K_PAD), spec(SAMPLE_K_PAD), spec(past), spec(past)],
        out_specs=spec(SAMPLE_Q_PAD),
        out_shape=jax.ShapeDtypeStruct((bsz, SAMPLE_Q_PAD, width), BF16),
        scratch_shapes=[pltpu.VMEM((past + SAMPLE_K_PAD, wg), BF16)] * 2,
        compiler_params=_params("parallel", "arbitrary"),
        name="attn_sample",
    )(q, k_new, v_new, k_cache, v_cache)


CONV_ROWS = 32


def _conv_taps(ubuf_ref, w_ref, cb_ref, out_ref, n_rows, base):
    for r0 in range(0, n_rows, CONV_ROWS):
        nr = min(CONV_ROWS, n_rows - r0)
        acc = jnp.broadcast_to(cb_ref[...], (nr, LANES))
        for j in range(CONV_W):
            acc = acc + ubuf_ref[pl.ds(base + r0 + j, nr), :] * w_ref[j:j + 1, :]
        out_ref[pl.ds(r0, nr), :] = acc


def _conv_prompt_kernel(ga_ref, gb_ref, w_ref, cb_ref, o_ref, ulast_ref, ubuf_ref):
    i = pl.program_id(1)
    tb = ga_ref.shape[0]

    @pl.when(i == 0)
    def _():
        ubuf_ref[0:CONV_HALO, :] = jnp.zeros((CONV_HALO, LANES), F32)

    @pl.when(i > 0)
    def _():
        ubuf_ref[0:CONV_HALO, :] = ubuf_ref[tb:tb + CONV_HALO, :]

    ubuf_ref[CONV_HALO:CONV_HALO + tb, :] = ga_ref[...] * jax.nn.sigmoid(gb_ref[...])
    _conv_taps(ubuf_ref, w_ref, cb_ref, o_ref, tb, CONV_HALO - (CONV_W - 1))
    ulast_ref[...] = ubuf_ref[tb:tb + CONV_HALO, :]


def _conv_prompt(proj3, t, slab_a, slab_b, n_slabs, conv_w, conv_b):
    tb = _pick(t, (256, 128))
    c = n_slabs * LANES
    return pl.pallas_call(
        _conv_prompt_kernel,
        grid=(n_slabs, t // tb),
        in_specs=[pl.BlockSpec((None, tb, LANES), lambda s, i: (slab_a + s, i, 0)),
                  pl.BlockSpec((None, tb, LANES), lambda s, i: (slab_b + s, i, 0)),
                  pl.BlockSpec((CONV_W, LANES), lambda s, i: (0, s)),
                  pl.BlockSpec((1, LANES), lambda s, i: (0, s))],
        out_specs=[pl.BlockSpec((tb, LANES), lambda s, i: (i, s)),
                   pl.BlockSpec((CONV_HALO, LANES), lambda s, i: (0, s))],
        out_shape=[jax.ShapeDtypeStruct((t, c), F32), jax.ShapeDtypeStruct((CONV_HALO, c), F32)],
        scratch_shapes=[pltpu.VMEM((CONV_HALO + tb, LANES), F32)],
        compiler_params=_params("parallel", "arbitrary"),
        name="conv_prompt",
    )(proj3, proj3, conv_w, conv_b.reshape(1, c))


SAMPLE_U_PAD = 8


def _conv_sample_kernel(st_ref, ga_ref, gb_ref, w_ref, cb_ref, o_ref, nst_ref, ubuf_ref, *, t_new):
    bsz, hist, _ = st_ref.shape

    def one_sequence(b, carry):
        ubuf_ref[0:hist, :] = st_ref[b]
        ubuf_ref[hist:hist + SAMPLE_U_PAD, :] = ga_ref[b] * jax.nn.sigmoid(gb_ref[b])
        _conv_taps(ubuf_ref, w_ref, cb_ref, o_ref.at[b], SAMPLE_U_PAD, 0)
        nst_ref[b] = ubuf_ref[pl.ds(t_new, hist), :]
        return carry

    lax.fori_loop(0, bsz, one_sequence, 0)


def _conv_sample(state, ga, gb, conv_w, conv_b, t_new):
    bsz, hist, c = state.shape
    assert hist == CONV_W - 1 and t_new <= SAMPLE_U_PAD
    spec = lambda rows: pl.BlockSpec((bsz, rows, LANES), lambda s: (0, 0, s))
    return pl.pallas_call(
        functools.partial(_conv_sample_kernel, t_new=t_new),
        grid=(c // LANES,),
        in_specs=[spec(hist), spec(SAMPLE_U_PAD), spec(SAMPLE_U_PAD),
                  pl.BlockSpec((CONV_W, LANES), lambda s: (0, s)),
                  pl.BlockSpec((1, LANES), lambda s: (0, s))],
        out_specs=[spec(SAMPLE_U_PAD), spec(hist)],
        out_shape=[jax.ShapeDtypeStruct((bsz, SAMPLE_U_PAD, c), F32),
                   jax.ShapeDtypeStruct((bsz, hist, c), F32)],
        scratch_shapes=[pltpu.VMEM((hist + SAMPLE_U_PAD + 2, LANES), F32)],
        compiler_params=_params("parallel"),
        name="conv_sample",
    )(state, ga, gb, conv_w, conv_b.reshape(1, c))


def _mem_attn_kernel(q_ref, k_ref, v_ref, o_ref):
    hd = q_ref.shape[-1]
    nt = (((1,), (1,)), ((), ()))
    s = lax.dot_general(q_ref[...].astype(BF16), k_ref[...].astype(BF16), nt,
                        preferred_element_type=F32) * (1.0 / math.sqrt(hd))
    m = jnp.max(s, axis=1, keepdims=True)
    p = jnp.exp(s - m)
    l = jnp.sum(p, axis=1, keepdims=True)
    o = jnp.dot(p.astype(BF16), v_ref[...].astype(BF16), preferred_element_type=F32)
    o_ref[...] = (o / l).astype(o_ref.dtype)


def _mem_attn(q, k, v, heads):
    bsz, tq_all, d = q.shape
    m = k.shape[1]
    hd = d // heads
    tq = _pick(tq_all, (512, 256, 128))
    qspec = pl.BlockSpec((None, tq, hd), lambda b, h, i: (b, i, h))
    kspec = pl.BlockSpec((None, m, hd), lambda b, h, i: (b, 0, h))
    return pl.pallas_call(
        _mem_attn_kernel,
        grid=(bsz, heads, tq_all // tq),
        in_specs=[qspec, kspec, kspec],
        out_specs=qspec,
        out_shape=jax.ShapeDtypeStruct((bsz, tq_all, d), BF16),
        compiler_params=_params("parallel", "parallel", "arbitrary"),
        name="mem_attn",
    )(q, k, v)


def _router_kernel(x_ref, w_ref, b_ref, eid_ref, gate_ref, *, n_groups, epg):
    x = x_ref[...]
    w = w_ref[...]
    x_hi = x.astype(BF16)
    x_lo = (x - x_hi.astype(F32)).astype(BF16)
    w_hi = w.astype(BF16)
    w_lo = (w - w_hi.astype(F32)).astype(BF16)
    dot = functools.partial(jnp.dot, preferred_element_type=F32)
    logits = dot(x_hi, w_hi) + (dot(x_hi, w_lo) + dot(x_lo, w_hi)) + b_ref[...]
    lane = lax.broadcasted_iota(jnp.int32, logits.shape, 1).astype(F32)
    neg = -jnp.inf
    none = float(LANES)

    def first_max(vals):
        top = jnp.max(vals, axis=1, keepdims=True)
        idx = jnp.min(jnp.where(vals == top, lane, none), axis=1, keepdims=True)
        return top, idx

    g_logits = jnp.where(lane < n_groups, logits, neg)
    g_top, grp = first_max(g_logits)
    p_grp = 1.0 / jnp.sum(jnp.exp(g_logits - g_top), axis=1, keepdims=True)
    lo = n_groups + grp * epg
    e_logits = jnp.where(jnp.logical_and(lane >= lo, lane < lo + epg), logits, neg)
    v1, i1 = first_max(e_logits)
    v2, i2 = first_max(jnp.where(lane == i1, neg, e_logits))
    e21 = jnp.exp(v2 - v1)
    g1 = p_grp * (1.0 / (1.0 + e21))
    g2 = p_grp * (e21 / (1.0 + e21))
    eid = jnp.where(lane == 0.0, i1 - n_groups, jnp.where(lane == 1.0, i2 - n_groups, 0.0))
    eid_ref[...] = eid.astype(jnp.int32)
    gate_ref[...] = jnp.where(lane == 0.0, g1, jnp.where(lane == 1.0, g2, 0.0))


def _router(x, w_rg, b_rg, w_re, b_re):
    n, d = x.shape
    n_groups, epg = w_re.shape[1], w_re.shape[2]
    n_log = n_groups + n_groups * epg
    assert n_log <= LANES
    w = jnp.concatenate([w_rg, w_re.reshape(d, n_groups * epg), jnp.zeros((d, LANES - n_log), F32)], axis=1)
    b = jnp.concatenate([b_rg, b_re.reshape(-1), jnp.zeros((LANES - n_log,), F32)]).reshape(1, LANES)
    tm = _pick(n, (640, 512, 256, 128))
    row = pl.BlockSpec((tm, LANES), lambda i: (i, 0))
    eid, gate = pl.pallas_call(
        functools.partial(_router_kernel, n_groups=n_groups, epg=epg),
        grid=(n // tm,),
        in_specs=[pl.BlockSpec((tm, d), lambda i: (i, 0)),
                  pl.BlockSpec((d, LANES), lambda i: (0, 0)),
                  pl.BlockSpec((1, LANES), lambda i: (0, 0))],
        out_specs=[row, row],
        out_shape=[jax.ShapeDtypeStruct((n, LANES), jnp.int32), jax.ShapeDtypeStruct((n, LANES), F32)],
        compiler_params=_params("parallel"),
        name="router",
    )(x, w, b)
    return eid[:, :2], gate


def _gather_rows_kernel(tok_ref, x_hbm, o_ref, sem):
    base = pl.program_id(0) * EXPERT_BLOCK

    def row_copy(r, src_row):
        return pltpu.make_async_copy(x_hbm.at[pl.ds(src_row, 1), :], o_ref.at[pl.ds(r, 1), :], sem)

    def start(r, carry):
        row_copy(r, tok_ref[base + r]).start()
        return carry

    def wait(r, carry):
        row_copy(r, 0).wait()
        return carry

    lax.fori_loop(0, EXPERT_BLOCK, start, 0)
    lax.fori_loop(0, EXPERT_BLOCK, wait, 0)


def _gather_rows(x, row_tok):
    n_rows = row_tok.shape[0]
    d = x.shape[1]
    return pl.pallas_call(
        _gather_rows_kernel,
        grid_spec=pltpu.PrefetchScalarGridSpec(
            num_scalar_prefetch=1,
            grid=(n_rows // EXPERT_BLOCK,),
            in_specs=[pl.BlockSpec(memory_space=pl.ANY)],
            out_specs=pl.BlockSpec((EXPERT_BLOCK, d), lambda b, tok: (b, 0)),
            scratch_shapes=[pltpu.SemaphoreType.DMA(())]),
        out_shape=jax.ShapeDtypeStruct((n_rows, d), x.dtype),
        compiler_params=_params("arbitrary"),
        name="gather_rows",
    )(row_tok, x)


BLOCK_VALID = 1
BLOCK_NEW_EXPERT = 2


def _expert_up_kernel(be_ref, fl_ref, x_ref, wg_ref, wu_ref, o_ref, wgb_ref, wub_ref):
    flags = fl_ref[pl.program_id(1)]

    @pl.when(flags >= BLOCK_NEW_EXPERT)
    def _():
        wgb_ref[...] = wg_ref[...].astype(BF16)
        wub_ref[...] = wu_ref[...].astype(BF16)

    @pl.when(jnp.bitwise_and(flags, BLOCK_VALID) != 0)
    def _():
        x = x_ref[...].astype(BF16)
        g = jnp.dot(x, wgb_ref[...], preferred_element_type=F32)
        u = jnp.dot(x, wub_ref[...], preferred_element_type=F32)
        o_ref[...] = (_silu(g) * u).astype(o_ref.dtype)

    @pl.when(jnp.bitwise_and(flags, BLOCK_VALID) == 0)
    def _():
        o_ref[...] = jnp.zeros(o_ref.shape, o_ref.dtype)


def _expert_up(xs, blk_e, blk_flags, w_gate, w_up):
    n_rows, d = xs.shape
    ff = w_gate.shape[2]
    tf = _pick(ff, (512, 256, 128))
    wspec = pl.BlockSpec((None, d, tf), lambda f, b, be, fl: (be[b], 0, f))
    return pl.pallas_call(
        _expert_up_kernel,
        grid_spec=pltpu.PrefetchScalarGridSpec(
            num_scalar_prefetch=2,
            grid=(ff // tf, n_rows // EXPERT_BLOCK),
            in_specs=[pl.BlockSpec((EXPERT_BLOCK, d), lambda f, b, be, fl: (b, 0)), wspec, wspec],
            out_specs=pl.BlockSpec((EXPERT_BLOCK, tf), lambda f, b, be, fl: (b, f)),
            scratch_shapes=[pltpu.VMEM((d, tf), BF16)] * 2),
        out_shape=jax.ShapeDtypeStruct((n_rows, ff), BF16),
        compiler_params=_params("arbitrary", "arbitrary"),
        name="expert_up",
    )(blk_e, blk_flags, xs, w_gate, w_up)


def _expert_down_kernel(be_ref, fl_ref, h_ref, wd_ref, o_ref, wdb_ref):
    flags = fl_ref[pl.program_id(1)]

    @pl.when(flags >= BLOCK_NEW_EXPERT)
    def _():
        wdb_ref[...] = wd_ref[...].astype(BF16)

    @pl.when(jnp.bitwise_and(flags, BLOCK_VALID) != 0)
    def _():
        o_ref[...] = jnp.dot(h_ref[...], wdb_ref[...], preferred_element_type=F32)

    @pl.when(jnp.bitwise_and(flags, BLOCK_VALID) == 0)
    def _():
        o_ref[...] = jnp.zeros(o_ref.shape, o_ref.dtype)


def _expert_down(hb, blk_e, blk_flags, w_down):
    n_rows, ff = hb.shape
    d = w_down.shape[2]
    tn = _pick(d, (2048, 1024, 512, 256, 128))
    return pl.pallas_call(
        _expert_down_kernel,
        grid_spec=pltpu.PrefetchScalarGridSpec(
            num_scalar_prefetch=2,
            grid=(d // tn, n_rows // EXPERT_BLOCK),
            in_specs=[pl.BlockSpec((EXPERT_BLOCK, ff), lambda n, b, be, fl: (b, 0)),
                      pl.BlockSpec((None, ff, tn), lambda n, b, be, fl: (be[b], 0, n))],
            out_specs=pl.BlockSpec((EXPERT_BLOCK, tn), lambda n, b, be, fl: (b, n)),
            scratch_shapes=[pltpu.VMEM((ff, tn), BF16)]),
        out_shape=jax.ShapeDtypeStruct((n_rows, d), F32),
        compiler_params=_params("arbitrary", "arbitrary"),
        name="expert_down",
    )(blk_e, blk_flags, hb, w_down)


def _combine_ln_kernel(d0_ref, d1_ref, ys_hbm, h_ref, gate_ref, g_ref, b_ref, o_ref, buf_ref, sem, *, alpha):
    tm = h_ref.shape[0]
    base = pl.program_id(0) * tm

    def row_copy(k, r, src_row):
        return pltpu.make_async_copy(ys_hbm.at[pl.ds(src_row, 1), :], buf_ref.at[k, pl.ds(r, 1), :], sem)

    def start(r, carry):
        row_copy(0, r, d0_ref[base + r]).start()
        row_copy(1, r, d1_ref[base + r]).start()
        return carry

    def wait(r, carry):
        row_copy(0, r, 0).wait()
        row_copy(1, r, 0).wait()
        return carry

    lax.fori_loop(0, tm, start, 0)
    lax.fori_loop(0, tm, wait, 0)
    ffn = buf_ref[0] * gate_ref[:, 0:1] + buf_ref[1] * gate_ref[:, 1:2]
    o_ref[...] = _ln_rows(alpha * h_ref[...] + ffn, g_ref[...], b_ref[...])


def _combine_ln(ys, dest0, dest1, h, gate, g, b, alpha):
    n, d = h.shape
    tm = _pick(n, (128, 64, 8))
    row = pl.BlockSpec((tm, d), lambda i, d0, d1: (i, 0))
    vec = pl.BlockSpec((1, d), lambda i, d0, d1: (0, 0))
    return pl.pallas_call(
        functools.partial(_combine_ln_kernel, alpha=alpha),
        grid_spec=pltpu.PrefetchScalarGridSpec(
            num_scalar_prefetch=2,
            grid=(n // tm,),
            in_specs=[pl.BlockSpec(memory_space=pl.ANY), row,
                      pl.BlockSpec((tm, LANES), lambda i, d0, d1: (i, 0)), vec, vec],
            out_specs=row,
            scratch_shapes=[pltpu.VMEM((2, tm, d), F32), pltpu.SemaphoreType.DMA(())]),
        out_shape=jax.ShapeDtypeStruct((n, d), F32),
        compiler_params=_params("arbitrary"),
        name="combine_ln",
    )(dest0, dest1, ys, h, gate, g.reshape(1, d), b.reshape(1, d))


def _dispatch_plan(eid, n_experts):
    n, k = eid.shape
    flat_e = eid.reshape(-1)
    onehot = (flat_e[:, None] == jnp.arange(n_experts, dtype=jnp.int32)[None, :]).astype(jnp.int32)
    csum = jnp.cumsum(onehot, axis=0)
    rank = jnp.sum(csum * onehot, axis=1) - 1
    counts = csum[-1]
    padded = (counts + EXPERT_BLOCK - 1) // EXPERT_BLOCK * EXPERT_BLOCK
    p_end = jnp.cumsum(padded)
    p_start = p_end - padded
    dest = (p_start[flat_e] + rank).astype(jnp.int32)
    n_blocks = -(-(n * k + n_experts * (EXPERT_BLOCK - 1)) // EXPERT_BLOCK)
    flat_tok = jnp.repeat(jnp.arange(n, dtype=jnp.int32), k)
    row_tok = jnp.zeros((n_blocks * EXPERT_BLOCK,), jnp.int32).at[dest].set(flat_tok)
    blk_row = jnp.arange(n_blocks, dtype=jnp.int32) * EXPERT_BLOCK
    blk_e = jnp.minimum(jnp.searchsorted(p_end, blk_row, side='right'), n_experts - 1).astype(jnp.int32)
    valid = blk_row < p_end[-1]
    new_e = jnp.concatenate([jnp.ones((1,), bool), blk_e[1:] != blk_e[:-1]])
    flags = (valid.astype(jnp.int32) * BLOCK_VALID + new_e.astype(jnp.int32) * BLOCK_NEW_EXPERT)
    return row_tok, blk_e, flags, dest.reshape(n, k)


def _pad_rows(a, rows):
    return jnp.pad(a, ((0, 0), (0, rows - a.shape[1]), (0, 0)))


def _layer(x_p, x_s, mem_prompt, win_k, win_v, conv_state, mem_k_s, mem_v_s,
           w_in, conv_w, conv_b, conv_ln_g, conv_ln_b, w_out, ln1_g, ln1_b,
           w_mem_q, w_mem_k, w_mem_v, w_mem_o, ln2_g, ln2_b,
           w_rg, b_rg, w_re, b_re, w_gate, w_up, w_down, ln3_g, ln3_b, alpha):
    bp, t, d = x_p.shape
    bs, ts, _ = x_s.shape
    past, heads = win_k.shape[1], win_k.shape[2]
    width = heads * HEAD_DIM
    conv_ch = conv_state.shape[2]
    cs = conv_ch // LANES
    mem_tokens, mem_heads = mem_k_s.shape[1], mem_k_s.shape[2]
    n_experts = w_gate.shape[0]
    assert bp == 1 and t % W_MAX == 0 and past == W_MAX and win_k.shape[3] == HEAD_DIM
    assert ts <= SAMPLE_U_PAD and w_in.shape[1] == 3 * width + 2 * conv_ch and width + conv_ch == d
    n_s = bs * ts

    x = jnp.concatenate([x_p.reshape(t, d), x_s.reshape(n_s, d)], axis=0)
    proj3 = _matmul_slabs(x, w_in.astype(BF16))

    def sample_part(lo, hi):
        return jnp.transpose(proj3[lo:hi, t:], (1, 0, 2)).reshape(bs, ts, (hi - lo) * LANES)

    q_s = sample_part(0, heads)
    k_s = sample_part(heads, 2 * heads)
    v_s = sample_part(2 * heads, 3 * heads)
    ga_s = sample_part(3 * heads, 3 * heads + cs)
    gb_s = sample_part(3 * heads + cs, 3 * heads + 2 * cs)

    attn_p = _attn_prompt(proj3, t, heads)
    attn_s = _attn_sample(_pad_rows(q_s, SAMPLE_Q_PAD), _pad_rows(k_s, SAMPLE_K_PAD), _pad_rows(v_s, SAMPLE_K_PAD),
                          win_k.reshape(bs, past, width), win_v.reshape(bs, past, width))[:, :ts]
    keep_p = min(W_MAX, t)
    win_k_p = jnp.transpose(proj3[heads:2 * heads, t - keep_p:t], (1, 0, 2))[None]
    win_v_p = jnp.transpose(proj3[2 * heads:3 * heads, t - keep_p:t], (1, 0, 2))[None]
    keep_s = min(W_MAX, past + ts)
    win_k_s = jnp.concatenate([win_k, k_s.reshape(bs, ts, heads, HEAD_DIM)], axis=1)[:, -keep_s:]
    win_v_s = jnp.concatenate([win_v, v_s.reshape(bs, ts, heads, HEAD_DIM)], axis=1)[:, -keep_s:]

    conv_p, u_last = _conv_prompt(proj3, t, 3 * heads, 3 * heads + cs, cs, conv_w, conv_b)
    conv_state_p = u_last[CONV_HALO - (CONV_W - 1):][None]
    conv_s, conv_state_s = _conv_sample(conv_state, _pad_rows(ga_s, SAMPLE_U_PAD), _pad_rows(gb_s, SAMPLE_U_PAD),
                                        conv_w, conv_b, ts)
    conv_all = jnp.concatenate([conv_p, conv_s[:, :ts].reshape(n_s, conv_ch)], axis=0)
    c_all = _ln_silu(conv_all, conv_ln_g, conv_ln_b)

    mixed = jnp.concatenate([jnp.concatenate([attn_p, attn_s.reshape(n_s, width)], axis=0), c_all], axis=1)
    h1, h1_bf = _res_ln(x, _matmul(mixed, w_out.astype(BF16)), ln1_g, ln1_b, alpha)

    qm = _matmul(h1_bf, w_mem_q.astype(BF16), out_dtype=BF16)
    mem_x = mem_prompt.reshape(bp * mem_tokens, d)
    mem_k_p = _matmul(mem_x, w_mem_k.astype(BF16))
    mem_v_p = _matmul(mem_x, w_mem_v.astype(BF16))
    om_p = _mem_attn(qm[:t][None], mem_k_p[None], mem_v_p[None], mem_heads)
    om_s = _mem_attn(_pad_rows(qm[t:].reshape(bs, ts, d), SAMPLE_Q_PAD), mem_k_s.reshape(bs, mem_tokens, d),
                     mem_v_s.reshape(bs, mem_tokens, d), mem_heads)[:, :ts]
    om = jnp.concatenate([om_p[0], om_s.reshape(n_s, d)], axis=0)
    h2, _ = _res_ln(h1, _matmul(om, w_mem_o.astype(BF16)), ln2_g, ln2_b, alpha)

    eid, gate = _router(h2, w_rg, b_rg, w_re, b_re)
    row_tok, blk_e, blk_flags, dest = _dispatch_plan(eid, n_experts)
    xs = _gather_rows(h2, row_tok)
    hb = _expert_up(xs, blk_e, blk_flags, w_gate, w_up)
    ys = _expert_down(hb, blk_e, blk_flags, w_down)
    y = _combine_ln(ys, dest[:, 0], dest[:, 1], h2, gate, ln3_g, ln3_b, alpha)

    mem_shape = (bp, mem_tokens, mem_heads, d // mem_heads)
    return (y[:t].reshape(bp, t, d), y[t:].reshape(bs, ts, d), win_k_p, win_v_p, conv_state_p,
            mem_k_p.reshape(mem_shape), mem_v_p.reshape(mem_shape), win_k_s, win_v_s, conv_state_s)


def kernel(x_prompt, x_sample, mem_prompt, cache_win_k, cache_win_v, state_conv, cache_mem_k, cache_mem_v, w_in, conv_w, conv_b, conv_ln_g, conv_ln_b, w_out, ln1_g, ln1_b, w_mem_q, w_mem_k, w_mem_v, w_mem_o, ln2_g, ln2_b, w_router_group, b_router_group, w_router_expert, b_router_expert, w_exp_gate, w_exp_up, w_exp_down, ln3_g, ln3_b):
    depth = w_in.shape[0]
    alpha = (2 * depth) ** 0.25
    hp, hs = x_prompt, x_sample
    per_layer = []
    for l in range(depth):
        outs = _layer(hp, hs, mem_prompt, cache_win_k[l], cache_win_v[l], state_conv[l], cache_mem_k[l], cache_mem_v[l],
                      w_in[l], conv_w[l], conv_b[l], conv_ln_g[l], conv_ln_b[l], w_out[l], ln1_g[l], ln1_b[l],
                      w_mem_q[l], w_mem_k[l], w_mem_v[l], w_mem_o[l], ln2_g[l], ln2_b[l],
                      w_router_group[l], b_router_group[l], w_router_expert[l], b_router_expert[l],
                      w_exp_gate[l], w_exp_up[l], w_exp_down[l], ln3_g[l], ln3_b[l], alpha)
        hp, hs = outs[0], outs[1]
        per_layer.append(outs[2:])
    stacked = [jnp.stack([layer[i] for layer in per_layer]) for i in range(8)]
    return (hp, hs, *stacked)
```

```python
import functools
import math

import jax
import jax.numpy as jnp
from jax import lax
from jax.experimental import pallas as pl
from jax.experimental.pallas import tpu as pltpu

F32 = jnp.float32
BF16 = jnp.bfloat16

LANES = 128
HEAD_DIM = 128
DILATED_BRANCHES = ((128, 1), (512, 4), (2048, 16))
W_MAX = 2048
KEYS_PER_BLOCK = 128
ATTN_GROUP = 4
CONV_W = 31
CONV_HALO = 32
EXPERT_BLOCK = 128
LN_EPS = 1e-5
VMEM_LIMIT = 56 * 1024 * 1024


def _params(*sem):
    return pltpu.CompilerParams(dimension_semantics=sem, vmem_limit_bytes=VMEM_LIMIT)


def _pick(n, candidates):
    for c in candidates:
        if n % c == 0:
            return c
    return n


def _mm_kernel(a_ref, b_ref, o_ref):
    o_ref[...] = jnp.dot(a_ref[...].astype(BF16), b_ref[...].astype(BF16),
                         preferred_element_type=F32).astype(o_ref.dtype)


def _matmul(a, b, out_dtype=F32):
    m, k = a.shape
    n = b.shape[1]
    tm = _pick(m, (640, 512, 256, 128))
    tn = _pick(n, (512, 256, 128))
    return pl.pallas_call(
        _mm_kernel,
        grid=(m // tm, n // tn),
        in_specs=[pl.BlockSpec((tm, k), lambda i, j: (i, 0)),
                  pl.BlockSpec((k, tn), lambda i, j: (0, j))],
        out_specs=pl.BlockSpec((tm, tn), lambda i, j: (i, j)),
        out_shape=jax.ShapeDtypeStruct((m, n), out_dtype),
        compiler_params=_params("parallel", "arbitrary"),
        name="matmul",
    )(a, b)


def _mm_slab_kernel(a_ref, b_ref, o_ref):
    r = jnp.dot(a_ref[...].astype(BF16), b_ref[...].astype(BF16), preferred_element_type=F32)
    for s in range(o_ref.shape[0]):
        o_ref[s] = r[:, s * LANES:(s + 1) * LANES]


def _matmul_slabs(a, b):
    m, k = a.shape
    n = b.shape[1]
    tm = _pick(m, (640, 512, 256, 128))
    tn = _pick(n, (512, 256, 128))
    return pl.pallas_call(
        _mm_slab_kernel,
        grid=(m // tm, n // tn),
        in_specs=[pl.BlockSpec((tm, k), lambda i, j: (i, 0)),
                  pl.BlockSpec((k, tn), lambda i, j: (0, j))],
        out_specs=pl.BlockSpec((tn // LANES, tm, LANES), lambda i, j: (j, i, 0)),
        out_shape=jax.ShapeDtypeStruct((n // LANES, m, LANES), F32),
        compiler_params=_params("parallel", "arbitrary"),
        name="matmul_slabs",
    )(a, b)


def _ln_rows(x, g, b):
    mu = jnp.mean(x, axis=-1, keepdims=True)
    xc = x - mu
    var = jnp.mean(xc * xc, axis=-1, keepdims=True)
    return xc * lax.rsqrt(var + LN_EPS) * g + b


def _silu(x):
    return x * jax.nn.sigmoid(x)


def _pack_bf16_pairs(y):
    half = y.shape[1] // 2
    hi = lax.bitcast_convert_type(y[:, :half].astype(BF16).astype(F32), jnp.uint32)
    lo = lax.bitcast_convert_type(y[:, half:].astype(BF16).astype(F32), jnp.uint32)
    return jnp.bitwise_or(hi, jnp.right_shift(lo, jnp.uint32(16)))


def _unpack_bf16_pairs(u):
    hi = lax.bitcast_convert_type(jnp.bitwise_and(u, jnp.uint32(0xFFFF0000)), F32)
    lo = lax.bitcast_convert_type(jnp.left_shift(u, jnp.uint32(16)), F32)
    return hi.astype(BF16), lo.astype(BF16)


def _res_ln_kernel(res_ref, t_ref, g_ref, b_ref, o_ref, o2_ref, *, alpha, pack):
    y = _ln_rows(alpha * res_ref[...] + t_ref[...], g_ref[...], b_ref[...])
    o_ref[...] = y
    o2_ref[...] = _pack_bf16_pairs(y) if pack else y.astype(BF16)


def _res_ln(res, t, g, b, alpha, pack=False):
    n, d = res.shape
    tm = _pick(n, (128, 64, 8))
    row = pl.BlockSpec((tm, d), lambda i: (i, 0))
    vec = pl.BlockSpec((1, d), lambda i: (0, 0))
    second = (jax.ShapeDtypeStruct((n, d // 2), jnp.uint32), pl.BlockSpec((tm, d // 2), lambda i: (i, 0))) if pack \
        else (jax.ShapeDtypeStruct((n, d), BF16), row)
    return pl.pallas_call(
        functools.partial(_res_ln_kernel, alpha=alpha, pack=pack),
        grid=(n // tm,),
        in_specs=[row, row, vec, vec],
        out_specs=[row, second[1]],
        out_shape=[jax.ShapeDtypeStruct((n, d), F32), second[0]],
        compiler_params=_params("parallel"),
        name="res_ln",
    )(res, t, g.reshape(1, d), b.reshape(1, d))


def _ln_silu_kernel(x_ref, g_ref, b_ref, o_ref):
    o_ref[...] = _silu(_ln_rows(x_ref[...], g_ref[...], b_ref[...])).astype(o_ref.dtype)


def _ln_silu(x, g, b):
    n, d = x.shape
    tm = _pick(n, (256, 128, 64, 8))
    row = pl.BlockSpec((tm, d), lambda i: (i, 0))
    vec = pl.BlockSpec((1, d), lambda i: (0, 0))
    return pl.pallas_call(
        _ln_silu_kernel,
        grid=(n // tm,),
        in_specs=[row, vec, vec],
        out_specs=row,
        out_shape=jax.ShapeDtypeStruct((n, d), BF16),
        compiler_params=_params("parallel"),
        name="ln_silu",
    )(x, g.reshape(1, d), b.reshape(1, d))


def _attn_prompt_kernel(q_ref, kp_ref, kc_ref, vp_ref, vc_ref, o_ref, acc_ref, m_ref, l_ref):
    c = pl.program_id(0)
    scale = 1.0 / math.sqrt(HEAD_DIM)
    blk = KEYS_PER_BLOCK
    ii = lax.broadcasted_iota(jnp.int32, (blk, blk), 0)
    jj = lax.broadcasted_iota(jnp.int32, (blk, blk), 1)
    nt = (((1,), (1,)), ((), ()))

    def rows(start, d):
        return pl.ds(start, blk) if d == 1 else pl.ds(start, blk, stride=d)

    def sub_blocks(subs, d, first, last):
        n = range(len(subs))
        cur = [rows(r + d * blk * b, d) for r, b in subs]
        prev = [rows(r + W_MAX - d * blk, d) if b == 0 else rows(r + d * blk * (b - 1), d) for r, b in subs]
        k_prev = [kp_ref if b == 0 else kc_ref for _, b in subs]
        v_prev = [vp_ref if b == 0 else vc_ref for _, b in subs]
        q = [(q_ref[cur[i], :] * scale).astype(BF16) for i in n]
        s_a = [lax.dot_general(q[i], k_prev[i][prev[i], :].astype(BF16), nt, preferred_element_type=F32) for i in n]
        s_b = [lax.dot_general(q[i], kc_ref[cur[i], :].astype(BF16), nt, preferred_element_type=F32) for i in n]
        for i, (_, b) in enumerate(subs):
            mask_a = jj >= ii
            if b == 0:
                mask_a = jnp.logical_and(mask_a, c > 0)
            s_a[i] = jnp.where(mask_a, s_a[i], -jnp.inf)
            s_b[i] = jnp.where(jj <= ii, s_b[i], -jnp.inf)
        m_loc = [jnp.maximum(jnp.max(s_a[i], axis=1, keepdims=True), jnp.max(s_b[i], axis=1, keepdims=True))
                 for i in n]
        p_a = [jnp.exp(s_a[i] - m_loc[i]) for i in n]
        p_b = [jnp.exp(s_b[i] - m_loc[i]) for i in n]
        l_loc = [jnp.sum(p_a[i], axis=1, keepdims=True) + jnp.sum(p_b[i], axis=1, keepdims=True) for i in n]
        acc_loc = [jnp.dot(p_a[i].astype(BF16), v_prev[i][prev[i], :].astype(BF16), preferred_element_type=F32)
                   + jnp.dot(p_b[i].astype(BF16), vc_ref[cur[i], :].astype(BF16), preferred_element_type=F32)
                   for i in n]
        for i in n:
            if first:
                m_new, l_new, acc_new = m_loc[i], l_loc[i], acc_loc[i]
            else:
                m_old = m_ref[cur[i], :][:, :1]
                l_old = l_ref[cur[i], :][:, :1]
                m_new = jnp.maximum(m_old, m_loc[i])
                a_old = jnp.exp(m_old - m_new)
                a_loc = jnp.exp(m_loc[i] - m_new)
                l_new = a_old * l_old + a_loc * l_loc[i]
                acc_new = a_old * acc_ref[cur[i], :] + a_loc * acc_loc[i]
            if last:
                o_ref[cur[i], :] = (acc_new / l_new).astype(o_ref.dtype)
            else:
                m_ref[cur[i], :] = jnp.broadcast_to(m_new, (blk, LANES))
                l_ref[cur[i], :] = jnp.broadcast_to(l_new, (blk, LANES))
                acc_ref[cur[i], :] = acc_new

    order = sorted(DILATED_BRANCHES, key=lambda wd: -wd[1])
    for idx, (w, d) in enumerate(order):
        subs = [(r, b) for b in range(W_MAX // (d * blk)) for r in range(d)]
        for g in range(0, len(subs), ATTN_GROUP):
            sub_blocks(subs[g:g + ATTN_GROUP], d, idx == 0, idx == len(order) - 1)


def _attn_prompt(proj3, t, heads):
    n_chunks = t // W_MAX
    blk = (None, W_MAX, LANES)
    cur = lambda off: pl.BlockSpec(blk, lambda c, h: (off + h, c, 0))
    prev = lambda off: pl.BlockSpec(blk, lambda c, h: (off + h, jnp.maximum(c - 1, 0), 0))
    return pl.pallas_call(
        _attn_prompt_kernel,
        grid=(n_chunks, heads),
        in_specs=[cur(0), prev(heads), cur(heads), prev(2 * heads), cur(2 * heads)],
        out_specs=pl.BlockSpec((W_MAX, LANES), lambda c, h: (c, h)),
        out_shape=jax.ShapeDtypeStruct((t, heads * LANES), BF16),
        scratch_shapes=[pltpu.VMEM((W_MAX, LANES), F32)] * 3,
        compiler_params=_params("parallel", "arbitrary"),
        name="attn_prompt",
    )(proj3, proj3, proj3, proj3, proj3)


SAMPLE_Q_PAD = 8
SAMPLE_HEADS = 8


def _key_multiplicity(dist):
    mult = jnp.zeros(dist.shape, F32)
    for (w, d) in DILATED_BRANCHES:
        hit = jnp.logical_and(jnp.bitwise_and(dist, d - 1) == 0, dist <= w)
        mult = mult + jnp.where(hit, 1.0, 0.0)
    return jnp.where(dist >= 0, mult, 0.0)


def _attn_sample_kernel(q_ref, kn_ref, vn_ref, kc_ref, vc_ref, o_ref, wk_hbm, wv_hbm, sem):
    b = pl.program_id(0)
    g = pl.program_id(1)
    _, past, hg, _ = kc_ref.shape
    t_new = kn_ref.shape[1]
    heads = pl.ds(pl.multiple_of(g * hg, hg), hg)
    copies = []
    for k, (c_ref, n_ref, w_hbm) in enumerate(((kc_ref, kn_ref, wk_hbm), (vc_ref, vn_ref, wv_hbm))):
        copies.append(pltpu.make_async_copy(c_ref.at[0, pl.ds(t_new, past - t_new)],
                                            w_hbm.at[b, pl.ds(0, past - t_new), heads, :], sem.at[2 * k]))
        copies.append(pltpu.make_async_copy(n_ref.at[0], w_hbm.at[b, pl.ds(past - t_new, t_new), heads, :],
                                            sem.at[2 * k + 1]))
    for cp in copies:
        cp.start()

    scale = 1.0 / math.sqrt(HEAD_DIM)
    qi = lax.broadcasted_iota(jnp.int32, (SAMPLE_Q_PAD, past), 0)
    rho = lax.broadcasted_iota(jnp.int32, (SAMPLE_Q_PAD, past), 1)
    mult_c = _key_multiplicity(past + qi - rho)
    qi1 = lax.broadcasted_iota(jnp.int32, (SAMPLE_Q_PAD, 1), 0)
    mult_n = [_key_multiplicity(qi1 - i) for i in range(t_new)]
    nt = (((1,), (1,)), ((), ()))
    k_rows = kc_ref.at[0].reshape(past * hg, HEAD_DIM)
    v_rows = vc_ref.at[0].reshape(past * hg, HEAD_DIM)
    for h in range(hg):
        sl = slice(h * HEAD_DIM, (h + 1) * HEAD_DIM)
        head_rows = pl.ds(h, past, stride=hg)
        q = q_ref[0, :, sl] * scale
        s_c = lax.dot_general(q.astype(BF16), k_rows[head_rows, :].astype(BF16), nt, preferred_element_type=F32)
        s_c = jnp.where(mult_c > 0, s_c, -jnp.inf)
        s_n = [jnp.sum(q * kn_ref[0, i, h:h + 1, :], axis=1, keepdims=True) for i in range(t_new)]
        s_n = [jnp.where(mult_n[i] > 0, s_n[i], -jnp.inf) for i in range(t_new)]
        m = jnp.max(s_c, axis=1, keepdims=True)
        for i in range(t_new):
            m = jnp.maximum(m, s_n[i])
        p_c = mult_c * jnp.exp(s_c - m)
        l = jnp.sum(p_c, axis=1, keepdims=True)
        o = jnp.dot(p_c.astype(BF16), v_rows[head_rows, :].astype(BF16), preferred_element_type=F32)
        for i in range(t_new):
            p_n = mult_n[i] * jnp.exp(s_n[i] - m)
            l = l + p_n
            o = o + p_n * vn_ref[0, i, h:h + 1, :]
        o_ref[0, :, sl] = o / l

    for cp in copies:
        cp.wait()


def _attn_sample(q, k_new, v_new, k_cache, v_cache):
    bsz, past, heads, _ = k_cache.shape
    t_new = k_new.shape[1]
    hg = SAMPLE_HEADS if heads % SAMPLE_HEADS == 0 else heads
    qspec = pl.BlockSpec((1, SAMPLE_Q_PAD, hg * HEAD_DIM), lambda b, g: (b, 0, g))
    kv = lambda rows: pl.BlockSpec((1, rows, hg, HEAD_DIM), lambda b, g: (b, 0, g, 0))
    win = jax.ShapeDtypeStruct(k_cache.shape, F32)
    return pl.pallas_call(
        _attn_sample_kernel,
        grid=(bsz, heads // hg),
        in_specs=[qspec, kv(t_new), kv(t_new), kv(past), kv(past)],
        out_specs=[qspec, pl.BlockSpec(memory_space=pl.ANY), pl.BlockSpec(memory_space=pl.ANY)],
        out_shape=[jax.ShapeDtypeStruct((bsz, SAMPLE_Q_PAD, heads * HEAD_DIM), F32), win, win],
        scratch_shapes=[pltpu.SemaphoreType.DMA((4,))],
        compiler_params=_params("arbitrary", "arbitrary"),
        name="attn_sample",
    )(q, k_new, v_new, k_cache, v_cache)


CONV_ROWS = 32


def _conv_taps(ubuf_ref, w_ref, cb_ref, out_ref, n_rows, base):
    for r0 in range(0, n_rows, CONV_ROWS):
        nr = min(CONV_ROWS, n_rows - r0)
        acc = jnp.broadcast_to(cb_ref[...], (nr, LANES))
        for j in range(CONV_W):
            acc = acc + ubuf_ref[pl.ds(base + r0 + j, nr), :] * w_ref[j:j + 1, :]
        out_ref[pl.ds(r0, nr), :] = acc


def _conv_prompt_kernel(ga_ref, gb_ref, w_ref, cb_ref, o_ref, ulast_ref, ubuf_ref):
    i = pl.program_id(1)
    tb = ga_ref.shape[0]

    @pl.when(i == 0)
    def _():
        ubuf_ref[0:CONV_HALO, :] = jnp.zeros((CONV_HALO, LANES), F32)

    @pl.when(i > 0)
    def _():
        ubuf_ref[0:CONV_HALO, :] = ubuf_ref[tb:tb + CONV_HALO, :]

    ubuf_ref[CONV_HALO:CONV_HALO + tb, :] = ga_ref[...] * jax.nn.sigmoid(gb_ref[...])
    _conv_taps(ubuf_ref, w_ref, cb_ref, o_ref, tb, CONV_HALO - (CONV_W - 1))
    ulast_ref[...] = ubuf_ref[tb:tb + CONV_HALO, :]


def _conv_prompt(proj3, t, slab_a, slab_b, n_slabs, conv_w, conv_b):
    tb = _pick(t, (256, 128))
    c = n_slabs * LANES
    return pl.pallas_call(
        _conv_prompt_kernel,
        grid=(n_slabs, t // tb),
        in_specs=[pl.BlockSpec((None, tb, LANES), lambda s, i: (slab_a + s, i, 0)),
                  pl.BlockSpec((None, tb, LANES), lambda s, i: (slab_b + s, i, 0)),
                  pl.BlockSpec((CONV_W, LANES), lambda s, i: (0, s)),
                  pl.BlockSpec((1, LANES), lambda s, i: (0, s))],
        out_specs=[pl.BlockSpec((tb, LANES), lambda s, i: (i, s)),
                   pl.BlockSpec((CONV_HALO, LANES), lambda s, i: (0, s))],
        out_shape=[jax.ShapeDtypeStruct((t, c), F32), jax.ShapeDtypeStruct((CONV_HALO, c), F32)],
        scratch_shapes=[pltpu.VMEM((CONV_HALO + tb, LANES), F32)],
        compiler_params=_params("parallel", "arbitrary"),
        name="conv_prompt",
    )(proj3, proj3, conv_w, conv_b.reshape(1, c))


SAMPLE_U_PAD = 8


def _conv_sample_kernel(st_ref, ga_ref, gb_ref, w_ref, cb_ref, o_ref, nst_ref, ubuf_ref, *, t_new):
    bsz, hist, _ = st_ref.shape

    def one_sequence(b, carry):
        ubuf_ref[0:hist, :] = st_ref[b]
        ubuf_ref[hist:hist + SAMPLE_U_PAD, :] = ga_ref[b] * jax.nn.sigmoid(gb_ref[b])
        _conv_taps(ubuf_ref, w_ref, cb_ref, o_ref.at[b], SAMPLE_U_PAD, 0)
        nst_ref[b] = ubuf_ref[pl.ds(t_new, hist), :]
        return carry

    lax.fori_loop(0, bsz, one_sequence, 0)


def _conv_sample(state, ga, gb, conv_w, conv_b, t_new):
    bsz, hist, c = state.shape
    assert hist == CONV_W - 1 and t_new <= SAMPLE_U_PAD
    spec = lambda rows: pl.BlockSpec((bsz, rows, LANES), lambda s: (0, 0, s))
    return pl.pallas_call(
        functools.partial(_conv_sample_kernel, t_new=t_new),
        grid=(c // LANES,),
        in_specs=[spec(hist), spec(SAMPLE_U_PAD), spec(SAMPLE_U_PAD),
                  pl.BlockSpec((CONV_W, LANES), lambda s: (0, s)),
                  pl.BlockSpec((1, LANES), lambda s: (0, s))],
        out_specs=[spec(SAMPLE_U_PAD), spec(hist)],
        out_shape=[jax.ShapeDtypeStruct((bsz, SAMPLE_U_PAD, c), F32),
                   jax.ShapeDtypeStruct((bsz, hist, c), F32)],
        scratch_shapes=[pltpu.VMEM((hist + SAMPLE_U_PAD + 2, LANES), F32)],
        compiler_params=_params("parallel"),
        name="conv_sample",
    )(state, ga, gb, conv_w, conv_b.reshape(1, c))


def _mem_attn_kernel(q_ref, k_ref, v_ref, o_ref):
    hd = q_ref.shape[-1]
    nt = (((1,), (1,)), ((), ()))
    s = lax.dot_general(q_ref[...].astype(BF16), k_ref[...].astype(BF16), nt,
                        preferred_element_type=F32) * (1.0 / math.sqrt(hd))
    m = jnp.max(s, axis=1, keepdims=True)
    p = jnp.exp(s - m)
    l = jnp.sum(p, axis=1, keepdims=True)
    o = jnp.dot(p.astype(BF16), v_ref[...].astype(BF16), preferred_element_type=F32)
    o_ref[...] = (o / l).astype(o_ref.dtype)


def _mem_attn(q, k, v, heads):
    bsz, tq_all, d = q.shape
    m = k.shape[1]
    hd = d // heads
    tq = _pick(tq_all, (512, 256, 128))
    qspec = pl.BlockSpec((None, tq, hd), lambda b, h, i: (b, i, h))
    kspec = pl.BlockSpec((None, m, hd), lambda b, h, i: (b, 0, h))
    return pl.pallas_call(
        _mem_attn_kernel,
        grid=(bsz, heads, tq_all // tq),
        in_specs=[qspec, kspec, kspec],
        out_specs=qspec,
        out_shape=jax.ShapeDtypeStruct((bsz, tq_all, d), BF16),
        compiler_params=_params("parallel", "parallel", "arbitrary"),
        name="mem_attn",
    )(q, k, v)


def _router_kernel(x_ref, w_ref, b_ref, eid_ref, gate_ref, *, n_groups, epg):
    x = x_ref[...]
    w = w_ref[...]
    x_hi = x.astype(BF16)
    x_lo = (x - x_hi.astype(F32)).astype(BF16)
    w_hi = w.astype(BF16)
    w_lo = (w - w_hi.astype(F32)).astype(BF16)
    dot = functools.partial(jnp.dot, preferred_element_type=F32)
    logits = dot(x_hi, w_hi) + (dot(x_hi, w_lo) + dot(x_lo, w_hi)) + b_ref[...]
    lane = lax.broadcasted_iota(jnp.int32, logits.shape, 1).astype(F32)
    neg = -jnp.inf
    none = float(LANES)

    def first_max(vals):
        top = jnp.max(vals, axis=1, keepdims=True)
        idx = jnp.min(jnp.where(vals == top, lane, none), axis=1, keepdims=True)
        return top, idx

    g_logits = jnp.where(lane < n_groups, logits, neg)
    g_top, grp = first_max(g_logits)
    p_grp = 1.0 / jnp.sum(jnp.exp(g_logits - g_top), axis=1, keepdims=True)
    lo = n_groups + grp * epg
    e_logits = jnp.where(jnp.logical_and(lane >= lo, lane < lo + epg), logits, neg)
    v1, i1 = first_max(e_logits)
    v2, i2 = first_max(jnp.where(lane == i1, neg, e_logits))
    e21 = jnp.exp(v2 - v1)
    g1 = p_grp * (1.0 / (1.0 + e21))
    g2 = p_grp * (e21 / (1.0 + e21))
    eid = jnp.where(lane == 0.0, i1 - n_groups, jnp.where(lane == 1.0, i2 - n_groups, 0.0))
    eid_ref[...] = eid.astype(jnp.int32)
    gate_ref[...] = jnp.where(lane == 0.0, g1, jnp.where(lane == 1.0, g2, 0.0))


def _router(x, w_rg, b_rg, w_re, b_re):
    n, d = x.shape
    n_groups, epg = w_re.shape[1], w_re.shape[2]
    n_log = n_groups + n_groups * epg
    assert n_log <= LANES
    w = jnp.concatenate([w_rg, w_re.reshape(d, n_groups * epg), jnp.zeros((d, LANES - n_log), F32)], axis=1)
    b = jnp.concatenate([b_rg, b_re.reshape(-1), jnp.zeros((LANES - n_log,), F32)]).reshape(1, LANES)
    tm = _pick(n, (640, 512, 256, 128))
    row = pl.BlockSpec((tm, LANES), lambda i: (i, 0))
    eid, gate = pl.pallas_call(
        functools.partial(_router_kernel, n_groups=n_groups, epg=epg),
        grid=(n // tm,),
        in_specs=[pl.BlockSpec((tm, d), lambda i: (i, 0)),
                  pl.BlockSpec((d, LANES), lambda i: (0, 0)),
                  pl.BlockSpec((1, LANES), lambda i: (0, 0))],
        out_specs=[row, row],
        out_shape=[jax.ShapeDtypeStruct((n, LANES), jnp.int32), jax.ShapeDtypeStruct((n, LANES), F32)],
        compiler_params=_params("parallel"),
        name="router",
    )(x, w, b)
    return eid[:, :2], gate


def _gather_rows_kernel(tok_ref, cnt_ref, x_hbm, o_ref, sem):
    b = pl.program_id(0)
    base = b * EXPERT_BLOCK
    cnt = cnt_ref[b]

    def row_copy(r, src_row):
        return pltpu.make_async_copy(x_hbm.at[pl.ds(src_row, 1), :], o_ref.at[pl.ds(r, 1), :], sem)

    def start_pair(i, carry):
        row_copy(2 * i, tok_ref[base + 2 * i]).start(priority=0)
        row_copy(2 * i + 1, tok_ref[base + 2 * i + 1]).start(priority=1)
        return carry

    def wait(r, carry):
        row_copy(r, 0).wait()
        return carry

    def zero_row(r, carry):
        o_ref[pl.ds(r, 1), :] = jnp.zeros((1, o_ref.shape[1]), o_ref.dtype)
        return carry

    lax.fori_loop(0, jnp.right_shift(cnt, 1), start_pair, 0)

    @pl.when(jnp.bitwise_and(cnt, 1) == 1)
    def _():
        row_copy(cnt - 1, tok_ref[base + cnt - 1]).start()

    lax.fori_loop(cnt, EXPERT_BLOCK, zero_row, 0)
    lax.fori_loop(0, cnt, wait, 0)


def _gather_rows(x, row_tok, blk_cnt):
    n_rows = row_tok.shape[0]
    d = x.shape[1]
    return pl.pallas_call(
        _gather_rows_kernel,
        grid_spec=pltpu.PrefetchScalarGridSpec(
            num_scalar_prefetch=2,
            grid=(n_rows // EXPERT_BLOCK,),
            in_specs=[pl.BlockSpec(memory_space=pl.ANY)],
            out_specs=pl.BlockSpec((EXPERT_BLOCK, d), lambda b, tok, cnt: (b, 0)),
            scratch_shapes=[pltpu.SemaphoreType.DMA(())]),
        out_shape=jax.ShapeDtypeStruct((n_rows, d), x.dtype),
        compiler_params=_params("arbitrary"),
        name="gather_rows",
    )(row_tok, blk_cnt, x)


BLOCK_VALID = 1
BLOCK_NEW_EXPERT = 2


def _stream_expert_weights(plan, w_hbms, wbuf_ref, wbf_ref, sem, tile):
    be_ref, fl_ref, nx_ref, seg_ref, nseg_ref = plan
    p = pl.program_id(0)
    b = pl.program_id(1)
    n_seg = nseg_ref[0]
    g = p * n_seg + seg_ref[b]
    slot = jnp.bitwise_and(g, 1)

    def tile_copies(expert, col_pass, dst_slot):
        cols = pl.ds(pl.multiple_of(col_pass * tile, tile), tile)
        return [pltpu.make_async_copy(w.at[expert, :, cols], wbuf_ref.at[dst_slot, k], sem.at[dst_slot, k])
                for k, w in enumerate(w_hbms)]

    @pl.when(fl_ref[b] >= BLOCK_NEW_EXPERT)
    def _():
        @pl.when(g == 0)
        def _():
            for cp in tile_copies(be_ref[b], p, slot):
                cp.start()

        for cp in tile_copies(be_ref[b], p, slot):
            cp.wait()

        @pl.when(g + 1 < pl.num_programs(0) * n_seg)
        def _():
            next_pass = jnp.where(seg_ref[b] + 1 == n_seg, p + 1, p)
            for cp in tile_copies(nx_ref[b], next_pass, 1 - slot):
                cp.start()

        for k in range(len(w_hbms)):
            wbf_ref[k] = wbuf_ref[slot, k].astype(BF16)


def _expert_up_kernel(be_ref, fl_ref, nx_ref, seg_ref, nseg_ref, x_ref, wg_hbm, wu_hbm, o_ref, wbuf_ref, wbf_ref, sem):
    _stream_expert_weights((be_ref, fl_ref, nx_ref, seg_ref, nseg_ref), (wg_hbm, wu_hbm), wbuf_ref, wbf_ref, sem,
                           o_ref.shape[1])
    valid = jnp.bitwise_and(fl_ref[pl.program_id(1)], BLOCK_VALID) != 0

    @pl.when(valid)
    def _():
        x_hi, x_lo = _unpack_bf16_pairs(x_ref[...])
        half = x_hi.shape[1]
        dot = functools.partial(jnp.dot, preferred_element_type=F32)
        g = dot(x_hi, wbf_ref[0, 0:half, :]) + dot(x_lo, wbf_ref[0, half:2 * half, :])
        u = dot(x_hi, wbf_ref[1, 0:half, :]) + dot(x_lo, wbf_ref[1, half:2 * half, :])
        o_ref[...] = (_silu(g) * u).astype(o_ref.dtype)

    @pl.when(jnp.logical_not(valid))
    def _():
        o_ref[...] = jnp.zeros(o_ref.shape, o_ref.dtype)


def _expert_up(xs, plan, w_gate, w_up):
    n_rows = xs.shape[0]
    d, ff = w_gate.shape[1], w_gate.shape[2]
    tf = _pick(ff, (512, 256, 128))
    return pl.pallas_call(
        _expert_up_kernel,
        grid_spec=pltpu.PrefetchScalarGridSpec(
            num_scalar_prefetch=5,
            grid=(ff // tf, n_rows // EXPERT_BLOCK),
            in_specs=[pl.BlockSpec((EXPERT_BLOCK, d // 2), lambda f, b, *_: (b, 0)),
                      pl.BlockSpec(memory_space=pl.ANY), pl.BlockSpec(memory_space=pl.ANY)],
            out_specs=pl.BlockSpec((EXPERT_BLOCK, tf), lambda f, b, *_: (b, f)),
            scratch_shapes=[pltpu.VMEM((2, 2, d, tf), F32), pltpu.VMEM((2, d, tf), BF16),
                            pltpu.SemaphoreType.DMA((2, 2))]),
        out_shape=jax.ShapeDtypeStruct((n_rows, ff), BF16),
        compiler_params=_params("arbitrary", "arbitrary"),
        name="expert_up",
    )(*plan, xs, w_gate, w_up)


def _expert_down_kernel(be_ref, fl_ref, nx_ref, seg_ref, nseg_ref, h_ref, wd_hbm, o_ref, wbuf_ref, wbf_ref, sem):
    _stream_expert_weights((be_ref, fl_ref, nx_ref, seg_ref, nseg_ref), (wd_hbm,), wbuf_ref, wbf_ref, sem,
                           o_ref.shape[1])
    valid = jnp.bitwise_and(fl_ref[pl.program_id(1)], BLOCK_VALID) != 0

    @pl.when(valid)
    def _():
        o_ref[...] = jnp.dot(h_ref[...], wbf_ref[0], preferred_element_type=F32)

    @pl.when(jnp.logical_not(valid))
    def _():
        o_ref[...] = jnp.zeros(o_ref.shape, o_ref.dtype)


def _expert_down(hb, plan, w_down):
    n_rows, ff = hb.shape
    d = w_down.shape[2]
    tn = _pick(d, (2048, 1024, 512, 256, 128))
    return pl.pallas_call(
        _expert_down_kernel,
        grid_spec=pltpu.PrefetchScalarGridSpec(
            num_scalar_prefetch=5,
            grid=(d // tn, n_rows // EXPERT_BLOCK),
            in_specs=[pl.BlockSpec((EXPERT_BLOCK, ff), lambda n, b, *_: (b, 0)),
                      pl.BlockSpec(memory_space=pl.ANY)],
            out_specs=pl.BlockSpec((EXPERT_BLOCK, tn), lambda n, b, *_: (b, n)),
            scratch_shapes=[pltpu.VMEM((2, 1, ff, tn), F32), pltpu.VMEM((1, ff, tn), BF16),
                            pltpu.SemaphoreType.DMA((2, 1))]),
        out_shape=jax.ShapeDtypeStruct((n_rows, d), F32),
        compiler_params=_params("arbitrary", "arbitrary"),
        name="expert_down",
    )(*plan, hb, w_down)


def _combine_ln_kernel(d0_ref, d1_ref, ys_hbm, h_ref, gate_ref, g_ref, b_ref, o_ref, buf_ref, sem, *, alpha):
    tm = h_ref.shape[0]
    base = pl.program_id(0) * tm

    def row_copy(k, r, src_row):
        return pltpu.make_async_copy(ys_hbm.at[pl.ds(src_row, 1), :], buf_ref.at[k, pl.ds(r, 1), :], sem)

    def start(r, carry):
        row_copy(0, r, d0_ref[base + r]).start()
        row_copy(1, r, d1_ref[base + r]).start()
        return carry

    def wait(r, carry):
        row_copy(0, r, 0).wait()
        row_copy(1, r, 0).wait()
        return carry

    lax.fori_loop(0, tm, start, 0)
    lax.fori_loop(0, tm, wait, 0)
    ffn = buf_ref[0] * gate_ref[:, 0:1] + buf_ref[1] * gate_ref[:, 1:2]
    o_ref[...] = _ln_rows(alpha * h_ref[...] + ffn, g_ref[...], b_ref[...])


def _combine_ln(ys, dest0, dest1, h, gate, g, b, alpha):
    n, d = h.shape
    tm = _pick(n, (128, 64, 8))
    row = pl.BlockSpec((tm, d), lambda i, d0, d1: (i, 0))
    vec = pl.BlockSpec((1, d), lambda i, d0, d1: (0, 0))
    return pl.pallas_call(
        functools.partial(_combine_ln_kernel, alpha=alpha),
        grid_spec=pltpu.PrefetchScalarGridSpec(
            num_scalar_prefetch=2,
            grid=(n // tm,),
            in_specs=[pl.BlockSpec(memory_space=pl.ANY), row,
                      pl.BlockSpec((tm, LANES), lambda i, d0, d1: (i, 0)), vec, vec],
            out_specs=row,
            scratch_shapes=[pltpu.VMEM((2, tm, d), F32), pltpu.SemaphoreType.DMA(())]),
        out_shape=jax.ShapeDtypeStruct((n, d), F32),
        compiler_params=_params("arbitrary"),
        name="combine_ln",
    )(dest0, dest1, ys, h, gate, g.reshape(1, d), b.reshape(1, d))


def _dispatch_plan(eid, n_experts):
    n, k = eid.shape
    flat_e = eid.reshape(-1)
    onehot = (flat_e[:, None] == jnp.arange(n_experts, dtype=jnp.int32)[None, :]).astype(jnp.int32)
    csum = jnp.cumsum(onehot, axis=0)
    rank = jnp.sum(csum * onehot, axis=1) - 1
    counts = csum[-1]
    padded = (counts + EXPERT_BLOCK - 1) // EXPERT_BLOCK * EXPERT_BLOCK
    p_end = jnp.cumsum(padded)
    p_start = p_end - padded
    dest = (p_start[flat_e] + rank).astype(jnp.int32)
    n_blocks = -(-(n * k + n_experts * (EXPERT_BLOCK - 1)) // EXPERT_BLOCK)
    flat_tok = jnp.repeat(jnp.arange(n, dtype=jnp.int32), k)
    row_tok = jnp.zeros((n_blocks * EXPERT_BLOCK,), jnp.int32).at[dest].set(flat_tok)
    blk_row = jnp.arange(n_blocks, dtype=jnp.int32) * EXPERT_BLOCK
    blk_e = jnp.minimum(jnp.searchsorted(p_end, blk_row, side='right'), n_experts - 1).astype(jnp.int32)
    valid = blk_row < p_end[-1]
    blk_e = jnp.where(valid, blk_e, blk_e[p_end[-1] // EXPERT_BLOCK - 1])
    blk_cnt = jnp.clip(counts[blk_e] - (blk_row - p_start[blk_e]), 0, EXPERT_BLOCK).astype(jnp.int32)
    new_e = jnp.concatenate([jnp.ones((1,), bool), blk_e[1:] != blk_e[:-1]])
    flags = (valid.astype(jnp.int32) * BLOCK_VALID + new_e.astype(jnp.int32) * BLOCK_NEW_EXPERT)
    seg = jnp.cumsum(new_e.astype(jnp.int32)) - 1
    n_seg = seg[-1:] + 1
    seg_e = jnp.zeros((n_experts,), jnp.int32).at[seg].set(blk_e)
    next_e = seg_e[(seg + 1) % n_seg]
    plan = (blk_e, flags, next_e.astype(jnp.int32), seg.astype(jnp.int32), n_seg.astype(jnp.int32))
    return row_tok, blk_cnt, plan, dest.reshape(n, k)


def _pad_rows(a, rows):
    return jnp.pad(a, ((0, 0), (0, rows - a.shape[1]), (0, 0)))


def _layer(x_p, x_s, mem_prompt, win_k, win_v, conv_state, mem_k_s, mem_v_s,
           w_in, conv_w, conv_b, conv_ln_g, conv_ln_b, w_out, ln1_g, ln1_b,
           w_mem_q, w_mem_k, w_mem_v, w_mem_o, ln2_g, ln2_b,
           w_rg, b_rg, w_re, b_re, w_gate, w_up, w_down, ln3_g, ln3_b, alpha):
    bp, t, d = x_p.shape
    bs, ts, _ = x_s.shape
    past, heads = win_k.shape[1], win_k.shape[2]
    width = heads * HEAD_DIM
    conv_ch = conv_state.shape[2]
    cs = conv_ch // LANES
    mem_tokens, mem_heads = mem_k_s.shape[1], mem_k_s.shape[2]
    n_experts = w_gate.shape[0]
    assert bp == 1 and t % W_MAX == 0 and past == W_MAX and win_k.shape[3] == HEAD_DIM
    assert ts <= SAMPLE_U_PAD and w_in.shape[1] == 3 * width + 2 * conv_ch and width + conv_ch == d
    n_s = bs * ts

    x = jnp.concatenate([x_p.reshape(t, d), x_s.reshape(n_s, d)], axis=0)
    proj3 = _matmul_slabs(x, w_in.astype(BF16))

    def sample_part(lo, hi):
        return jnp.transpose(proj3[lo:hi, t:], (1, 0, 2)).reshape(bs, ts, (hi - lo) * LANES)

    q_s = sample_part(0, heads)
    k_s = sample_part(heads, 2 * heads)
    v_s = sample_part(2 * heads, 3 * heads)
    ga_s = sample_part(3 * heads, 3 * heads + cs)
    gb_s = sample_part(3 * heads + cs, 3 * heads + 2 * cs)

    attn_p = _attn_prompt(proj3, t, heads)
    attn_s, win_k_s, win_v_s = _attn_sample(_pad_rows(q_s, SAMPLE_Q_PAD), k_s.reshape(bs, ts, heads, HEAD_DIM),
                                            v_s.reshape(bs, ts, heads, HEAD_DIM), win_k, win_v)
    attn_s = attn_s[:, :ts].astype(BF16)
    keep_p = min(W_MAX, t)
    win_k_p = jnp.transpose(proj3[heads:2 * heads, t - keep_p:t], (1, 0, 2))[None]
    win_v_p = jnp.transpose(proj3[2 * heads:3 * heads, t - keep_p:t], (1, 0, 2))[None]

    conv_p, u_last = _conv_prompt(proj3, t, 3 * heads, 3 * heads + cs, cs, conv_w, conv_b)
    conv_state_p = u_last[CONV_HALO - (CONV_W - 1):][None]
    conv_s, conv_state_s = _conv_sample(conv_state, _pad_rows(ga_s, SAMPLE_U_PAD), _pad_rows(gb_s, SAMPLE_U_PAD),
                                        conv_w, conv_b, ts)
    conv_all = jnp.concatenate([conv_p, conv_s[:, :ts].reshape(n_s, conv_ch)], axis=0)
    c_all = _ln_silu(conv_all, conv_ln_g, conv_ln_b)

    mixed = jnp.concatenate([jnp.concatenate([attn_p, attn_s.reshape(n_s, width)], axis=0), c_all], axis=1)
    h1, h1_bf = _res_ln(x, _matmul(mixed, w_out.astype(BF16)), ln1_g, ln1_b, alpha)

    qm = _matmul(h1_bf, w_mem_q.astype(BF16), out_dtype=BF16)
    mem_x = mem_prompt.reshape(bp * mem_tokens, d)
    mem_k_p = _matmul(mem_x, w_mem_k.astype(BF16))
    mem_v_p = _matmul(mem_x, w_mem_v.astype(BF16))
    om_p = _mem_attn(qm[:t][None], mem_k_p[None], mem_v_p[None], mem_heads)
    om_s = _mem_attn(_pad_rows(qm[t:].reshape(bs, ts, d), SAMPLE_Q_PAD), mem_k_s.reshape(bs, mem_tokens, d),
                     mem_v_s.reshape(bs, mem_tokens, d), mem_heads)[:, :ts]
    om = jnp.concatenate([om_p[0], om_s.reshape(n_s, d)], axis=0)
    h2, h2_packed = _res_ln(h1, _matmul(om, w_mem_o.astype(BF16)), ln2_g, ln2_b, alpha, pack=True)

    eid, gate = _router(h2, w_rg, b_rg, w_re, b_re)
    row_tok, blk_cnt, plan, dest = _dispatch_plan(eid, n_experts)
    xs = _gather_rows(h2_packed, row_tok, blk_cnt)
    hb = _expert_up(xs, plan, w_gate, w_up)
    ys = _expert_down(hb, plan, w_down)
    y = _combine_ln(ys, dest[:, 0], dest[:, 1], h2, gate, ln3_g, ln3_b, alpha)

    mem_shape = (bp, mem_tokens, mem_heads, d // mem_heads)
    return (y[:t].reshape(bp, t, d), y[t:].reshape(bs, ts, d), win_k_p, win_v_p, conv_state_p,
            mem_k_p.reshape(mem_shape), mem_v_p.reshape(mem_shape), win_k_s, win_v_s, conv_state_s)


def kernel(x_prompt, x_sample, mem_prompt, cache_win_k, cache_win_v, state_conv, cache_mem_k, cache_mem_v, w_in, conv_w, conv_b, conv_ln_g, conv_ln_b, w_out, ln1_g, ln1_b, w_mem_q, w_mem_k, w_mem_v, w_mem_o, ln2_g, ln2_b, w_router_group, b_router_group, w_router_expert, b_router_expert, w_exp_gate, w_exp_up, w_exp_down, ln3_g, ln3_b):
    depth = w_in.shape[0]
    alpha = (2 * depth) ** 0.25
    hp, hs = x_prompt, x_sample
    per_layer = []
    for l in range(depth):
        outs = _layer(hp, hs, mem_prompt, cache_win_k[l], cache_win_v[l], state_conv[l], cache_mem_k[l], cache_mem_v[l],
                      w_in[l], conv_w[l], conv_b[l], conv_ln_g[l], conv_ln_b[l], w_out[l], ln1_g[l], ln1_b[l],
                      w_mem_q[l], w_mem_k[l], w_mem_v[l], w_mem_o[l], ln2_g[l], ln2_b[l],
                      w_router_group[l], b_router_group[l], w_router_expert[l], b_router_expert[l],
                      w_exp_gate[l], w_exp_up[l], w_exp_down[l], ln3_g[l], ln3_b[l], alpha)
        hp, hs = outs[0], outs[1]
        per_layer.append(outs[2:])
    stacked = [jnp.stack([layer[i] for layer in per_layer]) for i in range(8)]
    return (hp, hs, *stacked)
```

```python
import functools
import math

import jax
import jax.numpy as jnp
from jax import lax
from jax.experimental import pallas as pl
from jax.experimental.pallas import tpu as pltpu

F32 = jnp.float32
BF16 = jnp.bfloat16

LANES = 128
HEAD_DIM = 128
DILATED_BRANCHES = ((128, 1), (512, 4), (2048, 16))
W_MAX = 2048
KEYS_PER_BLOCK = 128
ATTN_GROUP = 4
CONV_W = 31
CONV_HALO = 32
EXPERT_BLOCK = 128
LN_EPS = 1e-5
VMEM_LIMIT = 56 * 1024 * 1024


def _params(*sem):
    return pltpu.CompilerParams(dimension_semantics=sem, vmem_limit_bytes=VMEM_LIMIT)


def _pick(n, candidates):
    for c in candidates:
        if n % c == 0:
            return c
    return n


def _mm_kernel(a_ref, b_ref, o_ref):
    o_ref[...] = jnp.dot(a_ref[...].astype(BF16), b_ref[...].astype(BF16),
                         preferred_element_type=F32).astype(o_ref.dtype)


def _matmul(a, b, out_dtype=F32):
    m, k = a.shape
    n = b.shape[1]
    tm = _pick(m, (640, 512, 256, 128))
    tn = _pick(n, (512, 256, 128))
    return pl.pallas_call(
        _mm_kernel,
        grid=(m // tm, n // tn),
        in_specs=[pl.BlockSpec((tm, k), lambda i, j: (i, 0)),
                  pl.BlockSpec((k, tn), lambda i, j: (0, j))],
        out_specs=pl.BlockSpec((tm, tn), lambda i, j: (i, j)),
        out_shape=jax.ShapeDtypeStruct((m, n), out_dtype),
        compiler_params=_params("parallel", "arbitrary"),
        name="matmul",
    )(a, b)


def _mm_slab_kernel(a_ref, b_ref, o_ref):
    r = jnp.dot(a_ref[...].astype(BF16), b_ref[...].astype(BF16), preferred_element_type=F32)
    for s in range(o_ref.shape[0]):
        o_ref[s] = r[:, s * LANES:(s + 1) * LANES]


def _matmul_slabs(a, b):
    m, k = a.shape
    n = b.shape[1]
    tm = _pick(m, (640, 512, 256, 128))
    tn = _pick(n, (512, 256, 128))
    return pl.pallas_call(
        _mm_slab_kernel,
        grid=(m // tm, n // tn),
        in_specs=[pl.BlockSpec((tm, k), lambda i, j: (i, 0)),
                  pl.BlockSpec((k, tn), lambda i, j: (0, j))],
        out_specs=pl.BlockSpec((tn // LANES, tm, LANES), lambda i, j: (j, i, 0)),
        out_shape=jax.ShapeDtypeStruct((n // LANES, m, LANES), F32),
        compiler_params=_params("parallel", "arbitrary"),
        name="matmul_slabs",
    )(a, b)


def _ln_rows(x, g, b):
    mu = jnp.mean(x, axis=-1, keepdims=True)
    xc = x - mu
    var = jnp.mean(xc * xc, axis=-1, keepdims=True)
    return xc * lax.rsqrt(var + LN_EPS) * g + b


def _silu(x):
    return x * jax.nn.sigmoid(x)


def _pack_bf16_pairs(hi, lo):
    hi = lax.bitcast_convert_type(hi.astype(BF16).astype(F32), jnp.uint32)
    lo = lax.bitcast_convert_type(lo.astype(BF16).astype(F32), jnp.uint32)
    return jnp.bitwise_or(hi, jnp.right_shift(lo, jnp.uint32(16)))


def _unpack_bf16_pairs(u):
    hi = lax.bitcast_convert_type(jnp.bitwise_and(u, jnp.uint32(0xFFFF0000)), F32)
    lo = lax.bitcast_convert_type(jnp.left_shift(u, jnp.uint32(16)), F32)
    return hi.astype(BF16), lo.astype(BF16)


def _matmul_ln_kernel(*refs, n_src, alpha, pack):
    a_refs = refs[:n_src]
    w_ref, res_ref, g_ref, b_ref, o_ref, o2_ref, acc_ref = refs[n_src:]
    n = pl.program_id(1)
    n_tiles, _, tn = acc_ref.shape
    acc = alpha * res_ref[...]
    k0 = 0
    for a_ref in a_refs:
        k1 = k0 + a_ref.shape[1]
        acc = acc + jnp.dot(a_ref[...], w_ref[k0:k1, :], preferred_element_type=F32)
        k0 = k1
    acc_ref[n] = acc

    @pl.when(n == n_tiles - 1)
    def _():
        d = n_tiles * tn
        cols = [slice(j * tn, (j + 1) * tn) for j in range(n_tiles)]
        row_sum = lambda ts: functools.reduce(lambda p, q: p + q, [jnp.sum(t, axis=1, keepdims=True) for t in ts])
        mu = row_sum([acc_ref[j] for j in range(n_tiles)]) * (1.0 / d)
        var = row_sum([jnp.square(acc_ref[j] - mu) for j in range(n_tiles)]) * (1.0 / d)
        inv = lax.rsqrt(var + LN_EPS)
        ys = [(acc_ref[j] - mu) * inv * g_ref[:, cols[j]] + b_ref[:, cols[j]] for j in range(n_tiles)]
        for j in range(n_tiles):
            o_ref[:, cols[j]] = ys[j]
            if not pack:
                o2_ref[:, cols[j]] = ys[j].astype(BF16)
        if pack:
            for j in range(n_tiles // 2):
                o2_ref[:, cols[j]] = _pack_bf16_pairs(ys[j], ys[j + n_tiles // 2])


def _matmul_ln(srcs, w, res, g, b, alpha, pack=False):
    n, d = res.shape
    tm = 512 if n >= 512 else n
    tn = _pick(d // 2 if pack else d, (256, 128))
    a_specs = [pl.BlockSpec((tm, a.shape[1]), lambda i, j: (i, 0)) for a in srcs]
    row = pl.BlockSpec((tm, d), lambda i, j: (i, 0))
    vec = pl.BlockSpec((1, d), lambda i, j: (0, 0))
    second = (jax.ShapeDtypeStruct((n, d // 2), jnp.uint32), pl.BlockSpec((tm, d // 2), lambda i, j: (i, 0))) if pack \
        else (jax.ShapeDtypeStruct((n, d), BF16), row)
    return pl.pallas_call(
        functools.partial(_matmul_ln_kernel, n_src=len(srcs), alpha=alpha, pack=pack),
        grid=(pl.cdiv(n, tm), d // tn),
        in_specs=a_specs + [pl.BlockSpec((w.shape[0], tn), lambda i, j: (0, j)),
                            pl.BlockSpec((tm, tn), lambda i, j: (i, j)), vec, vec],
        out_specs=[row, second[1]],
        out_shape=[jax.ShapeDtypeStruct((n, d), F32), second[0]],
        scratch_shapes=[pltpu.VMEM((d // tn, tm, tn), F32)],
        compiler_params=_params("parallel", "arbitrary"),
        name="matmul_ln",
    )(*srcs, w, res, g.reshape(1, d), b.reshape(1, d))


def _ln_silu_kernel(x_ref, g_ref, b_ref, o_ref):
    o_ref[...] = _silu(_ln_rows(x_ref[...], g_ref[...], b_ref[...])).astype(o_ref.dtype)


def _ln_silu(x, g, b):
    n, d = x.shape
    tm = _pick(n, (256, 128, 64, 8))
    row = pl.BlockSpec((tm, d), lambda i: (i, 0))
    vec = pl.BlockSpec((1, d), lambda i: (0, 0))
    return pl.pallas_call(
        _ln_silu_kernel,
        grid=(n // tm,),
        in_specs=[row, vec, vec],
        out_specs=row,
        out_shape=jax.ShapeDtypeStruct((n, d), BF16),
        compiler_params=_params("parallel"),
        name="ln_silu",
    )(x, g.reshape(1, d), b.reshape(1, d))


def _attn_prompt_kernel(q_ref, kp_ref, kc_ref, vp_ref, vc_ref, o_ref, acc_ref, m_ref, l_ref, *, n_chunks):
    c = pl.program_id(0)
    scale = 1.0 / math.sqrt(HEAD_DIM)
    blk = KEYS_PER_BLOCK
    ii = lax.broadcasted_iota(jnp.int32, (blk, blk), 0)
    jj = lax.broadcasted_iota(jnp.int32, (blk, blk), 1)
    nt = (((1,), (1,)), ((), ()))

    def rows(start, d):
        return pl.ds(start, blk) if d == 1 else pl.ds(start, blk, stride=d)

    def sub_blocks(subs, d, first, last):
        n = range(len(subs))
        cur = [rows(r + d * blk * b, d) for r, b in subs]
        prev = [rows(r + W_MAX - d * blk, d) if b == 0 else rows(r + d * blk * (b - 1), d) for r, b in subs]
        k_prev = [kp_ref if b == 0 else kc_ref for _, b in subs]
        v_prev = [vp_ref if b == 0 else vc_ref for _, b in subs]
        q = [(q_ref[cur[i], :] * scale).astype(BF16) for i in n]
        s_a = [lax.dot_general(q[i], k_prev[i][prev[i], :].astype(BF16), nt, preferred_element_type=F32) for i in n]
        s_b = [lax.dot_general(q[i], kc_ref[cur[i], :].astype(BF16), nt, preferred_element_type=F32) for i in n]
        for i, (_, b) in enumerate(subs):
            mask_a = jj >= ii
            if b == 0:
                mask_a = jnp.logical_and(mask_a, c > 0)
            s_a[i] = jnp.where(mask_a, s_a[i], -jnp.inf)
            s_b[i] = jnp.where(jj <= ii, s_b[i], -jnp.inf)
        m_loc = [jnp.maximum(jnp.max(s_a[i], axis=1, keepdims=True), jnp.max(s_b[i], axis=1, keepdims=True))
                 for i in n]
        p_a = [jnp.exp(s_a[i] - m_loc[i]) for i in n]
        p_b = [jnp.exp(s_b[i] - m_loc[i]) for i in n]
        l_loc = [jnp.sum(p_a[i], axis=1, keepdims=True) + jnp.sum(p_b[i], axis=1, keepdims=True) for i in n]
        acc_loc = [jnp.dot(p_a[i].astype(BF16), v_prev[i][prev[i], :].astype(BF16), preferred_element_type=F32)
                   + jnp.dot(p_b[i].astype(BF16), vc_ref[cur[i], :].astype(BF16), preferred_element_type=F32)
                   for i in n]
        for i in n:
            if first:
                m_new, l_new, acc_new = m_loc[i], l_loc[i], acc_loc[i]
            else:
                m_old = m_ref[cur[i], :][:, :1]
                l_old = l_ref[cur[i], :][:, :1]
                m_new = jnp.maximum(m_old, m_loc[i])
                a_old = jnp.exp(m_old - m_new)
                a_loc = jnp.exp(m_loc[i] - m_new)
                l_new = a_old * l_old + a_loc * l_loc[i]
                acc_new = a_old * acc_ref[cur[i], :] + a_loc * acc_loc[i]
            if last:
                o_ref[cur[i], :] = (acc_new / l_new).astype(o_ref.dtype)
            else:
                m_ref[cur[i], :] = jnp.broadcast_to(m_new, (blk, LANES))
                l_ref[cur[i], :] = jnp.broadcast_to(l_new, (blk, LANES))
                acc_ref[cur[i], :] = acc_new

    @pl.when(c < n_chunks)
    def _():
        order = sorted(DILATED_BRANCHES, key=lambda wd: -wd[1])
        for idx, (w, d) in enumerate(order):
            subs = [(r, b) for b in range(W_MAX // (d * blk)) for r in range(d)]
            for g in range(0, len(subs), ATTN_GROUP):
                sub_blocks(subs[g:g + ATTN_GROUP], d, idx == 0, idx == len(order) - 1)

    @pl.when(c >= n_chunks)
    def _():
        o_ref[...] = jnp.zeros(o_ref.shape, o_ref.dtype)


def _attn_prompt(proj3, t, heads):
    n_total = proj3.shape[1]
    n_chunks = t // W_MAX
    blk = (None, W_MAX, LANES)
    chunk = lambda c: jnp.minimum(c, n_chunks - 1)
    cur = lambda off: pl.BlockSpec(blk, lambda c, h: (off + h, chunk(c), 0))
    prev = lambda off: pl.BlockSpec(blk, lambda c, h: (off + h, jnp.maximum(chunk(c) - 1, 0), 0))
    return pl.pallas_call(
        functools.partial(_attn_prompt_kernel, n_chunks=n_chunks),
        grid=(pl.cdiv(n_total, W_MAX), heads),
        in_specs=[cur(0), prev(heads), cur(heads), prev(2 * heads), cur(2 * heads)],
        out_specs=pl.BlockSpec((W_MAX, LANES), lambda c, h: (c, h)),
        out_shape=jax.ShapeDtypeStruct((n_total, heads * LANES), BF16),
        scratch_shapes=[pltpu.VMEM((W_MAX, LANES), F32)] * 3,
        compiler_params=_params("parallel", "arbitrary"),
        name="attn_prompt",
    )(proj3, proj3, proj3, proj3, proj3)


SAMPLE_Q_PAD = 8
SAMPLE_HEADS = 8


def _key_multiplicity(dist):
    mult = jnp.zeros(dist.shape, F32)
    for (w, d) in DILATED_BRANCHES:
        hit = jnp.logical_and(jnp.bitwise_and(dist, d - 1) == 0, dist <= w)
        mult = mult + jnp.where(hit, 1.0, 0.0)
    return jnp.where(dist >= 0, mult, 0.0)


def _attn_sample_kernel(q_ref, kn_ref, vn_ref, kc_ref, vc_ref, o_ref, wk_hbm, wv_hbm, sem):
    b = pl.program_id(0)
    g = pl.program_id(1)
    _, past, hg, _ = kc_ref.shape
    t_new = kn_ref.shape[1]
    heads = pl.ds(pl.multiple_of(g * hg, hg), hg)
    copies = []
    for k, (c_ref, n_ref, w_hbm) in enumerate(((kc_ref, kn_ref, wk_hbm), (vc_ref, vn_ref, wv_hbm))):
        copies.append(pltpu.make_async_copy(c_ref.at[0, pl.ds(t_new, past - t_new)],
                                            w_hbm.at[b, pl.ds(0, past - t_new), heads, :], sem.at[2 * k]))
        copies.append(pltpu.make_async_copy(n_ref.at[0], w_hbm.at[b, pl.ds(past - t_new, t_new), heads, :],
                                            sem.at[2 * k + 1]))
    for cp in copies:
        cp.start()

    scale = 1.0 / math.sqrt(HEAD_DIM)
    qi = lax.broadcasted_iota(jnp.int32, (SAMPLE_Q_PAD, past), 0)
    rho = lax.broadcasted_iota(jnp.int32, (SAMPLE_Q_PAD, past), 1)
    mult_c = _key_multiplicity(past + qi - rho)
    qi1 = lax.broadcasted_iota(jnp.int32, (SAMPLE_Q_PAD, 1), 0)
    mult_n = [_key_multiplicity(qi1 - i) for i in range(t_new)]
    nt = (((1,), (1,)), ((), ()))
    k_rows = kc_ref.at[0].reshape(past * hg, HEAD_DIM)
    v_rows = vc_ref.at[0].reshape(past * hg, HEAD_DIM)
    for h in range(hg):
        sl = slice(h * HEAD_DIM, (h + 1) * HEAD_DIM)
        head_rows = pl.ds(h, past, stride=hg)
        q = q_ref[0, :, sl] * scale
        s_c = lax.dot_general(q.astype(BF16), k_rows[head_rows, :].astype(BF16), nt, preferred_element_type=F32)
        s_c = jnp.where(mult_c > 0, s_c, -jnp.inf)
        s_n = [jnp.sum(q * kn_ref[0, i, h:h + 1, :], axis=1, keepdims=True) for i in range(t_new)]
        s_n = [jnp.where(mult_n[i] > 0, s_n[i], -jnp.inf) for i in range(t_new)]
        m = jnp.max(s_c, axis=1, keepdims=True)
        for i in range(t_new):
            m = jnp.maximum(m, s_n[i])
        p_c = mult_c * jnp.exp(s_c - m)
        l = jnp.sum(p_c, axis=1, keepdims=True)
        o = jnp.dot(p_c.astype(BF16), v_rows[head_rows, :].astype(BF16), preferred_element_type=F32)
        for i in range(t_new):
            p_n = mult_n[i] * jnp.exp(s_n[i] - m)
            l = l + p_n
            o = o + p_n * vn_ref[0, i, h:h + 1, :]
        o_ref[0, :, sl] = o / l

    for cp in copies:
        cp.wait()


def _attn_sample(q, k_new, v_new, k_cache, v_cache):
    bsz, past, heads, _ = k_cache.shape
    t_new = k_new.shape[1]
    hg = SAMPLE_HEADS if heads % SAMPLE_HEADS == 0 else heads
    qspec = pl.BlockSpec((1, SAMPLE_Q_PAD, hg * HEAD_DIM), lambda b, g: (b, 0, g))
    kv = lambda rows: pl.BlockSpec((1, rows, hg, HEAD_DIM), lambda b, g: (b, 0, g, 0))
    win = jax.ShapeDtypeStruct(k_cache.shape, F32)
    return pl.pallas_call(
        _attn_sample_kernel,
        grid=(bsz, heads // hg),
        in_specs=[qspec, kv(t_new), kv(t_new), kv(past), kv(past)],
        out_specs=[qspec, pl.BlockSpec(memory_space=pl.ANY), pl.BlockSpec(memory_space=pl.ANY)],
        out_shape=[jax.ShapeDtypeStruct((bsz, SAMPLE_Q_PAD, heads * HEAD_DIM), F32), win, win],
        scratch_shapes=[pltpu.SemaphoreType.DMA((4,))],
        compiler_params=_params("arbitrary", "arbitrary"),
        name="attn_sample",
    )(q, k_new, v_new, k_cache, v_cache)


CONV_ROWS = 32


def _conv_taps(ubuf_ref, w_ref, cb_ref, out_ref, n_rows, base):
    for r0 in range(0, n_rows, CONV_ROWS):
        nr = min(CONV_ROWS, n_rows - r0)
        acc = jnp.broadcast_to(cb_ref[...], (nr, LANES))
        for j in range(CONV_W):
            acc = acc + ubuf_ref[pl.ds(base + r0 + j, nr), :] * w_ref[j:j + 1, :]
        out_ref[pl.ds(r0, nr), :] = acc


def _conv_prompt_kernel(ga_ref, gb_ref, w_ref, cb_ref, o_ref, ulast_ref, ubuf_ref, *, n_blocks):
    i = pl.program_id(1)
    tb = ga_ref.shape[0]

    @pl.when(i == 0)
    def _():
        ubuf_ref[0:CONV_HALO, :] = jnp.zeros((CONV_HALO, LANES), F32)

    @pl.when(jnp.logical_and(i > 0, i < n_blocks))
    def _():
        ubuf_ref[0:CONV_HALO, :] = ubuf_ref[tb:tb + CONV_HALO, :]

    @pl.when(i < n_blocks)
    def _():
        ubuf_ref[CONV_HALO:CONV_HALO + tb, :] = ga_ref[...] * jax.nn.sigmoid(gb_ref[...])
        _conv_taps(ubuf_ref, w_ref, cb_ref, o_ref, tb, CONV_HALO - (CONV_W - 1))

    @pl.when(i == n_blocks - 1)
    def _():
        ulast_ref[...] = ubuf_ref[tb:tb + CONV_HALO, :]

    @pl.when(i >= n_blocks)
    def _():
        o_ref[...] = jnp.zeros(o_ref.shape, o_ref.dtype)


def _conv_prompt(proj3, t, slab_a, slab_b, n_slabs, conv_w, conv_b):
    n_total = proj3.shape[1]
    tb = _pick(t, (256, 128))
    c = n_slabs * LANES
    n_blocks = t // tb
    blk = lambda i: jnp.minimum(i, n_blocks - 1)
    return pl.pallas_call(
        functools.partial(_conv_prompt_kernel, n_blocks=n_blocks),
        grid=(n_slabs, pl.cdiv(n_total, tb)),
        in_specs=[pl.BlockSpec((None, tb, LANES), lambda s, i: (slab_a + s, blk(i), 0)),
                  pl.BlockSpec((None, tb, LANES), lambda s, i: (slab_b + s, blk(i), 0)),
                  pl.BlockSpec((CONV_W, LANES), lambda s, i: (0, s)),
                  pl.BlockSpec((1, LANES), lambda s, i: (0, s))],
        out_specs=[pl.BlockSpec((tb, LANES), lambda s, i: (i, s)),
                   pl.BlockSpec((CONV_HALO, LANES), lambda s, i: (0, s))],
        out_shape=[jax.ShapeDtypeStruct((n_total, c), F32), jax.ShapeDtypeStruct((CONV_HALO, c), F32)],
        scratch_shapes=[pltpu.VMEM((CONV_HALO + tb, LANES), F32)],
        compiler_params=_params("parallel", "arbitrary"),
        name="conv_prompt",
    )(proj3, proj3, conv_w, conv_b.reshape(1, c))


SAMPLE_U_PAD = 8


def _conv_sample_kernel(st_ref, ga_ref, gb_ref, w_ref, cb_ref, o_ref, nst_ref, ubuf_ref, *, t_new):
    bsz, hist, _ = st_ref.shape

    def one_sequence(b, carry):
        ubuf_ref[0:hist, :] = st_ref[b]
        ubuf_ref[hist:hist + SAMPLE_U_PAD, :] = ga_ref[b] * jax.nn.sigmoid(gb_ref[b])
        _conv_taps(ubuf_ref, w_ref, cb_ref, o_ref.at[b], SAMPLE_U_PAD, 0)
        nst_ref[b] = ubuf_ref[pl.ds(t_new, hist), :]
        return carry

    lax.fori_loop(0, bsz, one_sequence, 0)


def _conv_sample(state, ga, gb, conv_w, conv_b, t_new):
    bsz, hist, c = state.shape
    assert hist == CONV_W - 1 and t_new <= SAMPLE_U_PAD
    spec = lambda rows: pl.BlockSpec((bsz, rows, LANES), lambda s: (0, 0, s))
    return pl.pallas_call(
        functools.partial(_conv_sample_kernel, t_new=t_new),
        grid=(c // LANES,),
        in_specs=[spec(hist), spec(SAMPLE_U_PAD), spec(SAMPLE_U_PAD),
                  pl.BlockSpec((CONV_W, LANES), lambda s: (0, s)),
                  pl.BlockSpec((1, LANES), lambda s: (0, s))],
        out_specs=[spec(SAMPLE_U_PAD), spec(hist)],
        out_shape=[jax.ShapeDtypeStruct((bsz, SAMPLE_U_PAD, c), F32),
                   jax.ShapeDtypeStruct((bsz, hist, c), F32)],
        scratch_shapes=[pltpu.VMEM((hist + SAMPLE_U_PAD + 2, LANES), F32)],
        compiler_params=_params("parallel"),
        name="conv_sample",
    )(state, ga, gb, conv_w, conv_b.reshape(1, c))


def _mem_attn_kernel(q_ref, k_ref, v_ref, o_ref):
    hd = q_ref.shape[-1]
    nt = (((1,), (1,)), ((), ()))
    s = lax.dot_general(q_ref[...].astype(BF16), k_ref[...].astype(BF16), nt,
                        preferred_element_type=F32) * (1.0 / math.sqrt(hd))
    m = jnp.max(s, axis=1, keepdims=True)
    p = jnp.exp(s - m)
    l = jnp.sum(p, axis=1, keepdims=True)
    o = jnp.dot(p.astype(BF16), v_ref[...].astype(BF16), preferred_element_type=F32)
    o_ref[...] = (o / l).astype(o_ref.dtype)


def _mem_attn(q, k, v, heads):
    bsz, tq_all, d = q.shape
    m = k.shape[1]
    hd = d // heads
    tq = 512 if tq_all >= 512 else tq_all
    qspec = pl.BlockSpec((None, tq, hd), lambda b, h, i: (b, i, h))
    kspec = pl.BlockSpec((None, m, hd), lambda b, h, i: (b, 0, h))
    return pl.pallas_call(
        _mem_attn_kernel,
        grid=(bsz, heads, pl.cdiv(tq_all, tq)),
        in_specs=[qspec, kspec, kspec],
        out_specs=qspec,
        out_shape=jax.ShapeDtypeStruct((bsz, tq_all, d), BF16),
        compiler_params=_params("parallel", "parallel", "arbitrary"),
        name="mem_attn",
    )(q, k, v)


def _router_kernel(x_ref, w_ref, b_ref, eid_ref, gate_ref, *, n_groups, epg):
    x = x_ref[...]
    w = w_ref[...]
    x_hi = x.astype(BF16)
    x_lo = (x - x_hi.astype(F32)).astype(BF16)
    w_hi = w.astype(BF16)
    w_lo = (w - w_hi.astype(F32)).astype(BF16)
    dot = functools.partial(jnp.dot, preferred_element_type=F32)
    logits = dot(x_hi, w_hi) + (dot(x_hi, w_lo) + dot(x_lo, w_hi)) + b_ref[...]
    lane = lax.broadcasted_iota(jnp.int32, logits.shape, 1).astype(F32)
    neg = -jnp.inf
    none = float(LANES)

    def first_max(vals):
        top = jnp.max(vals, axis=1, keepdims=True)
        idx = jnp.min(jnp.where(vals == top, lane, none), axis=1, keepdims=True)
        return top, idx

    g_logits = jnp.where(lane < n_groups, logits, neg)
    g_top, grp = first_max(g_logits)
    p_grp = 1.0 / jnp.sum(jnp.exp(g_logits - g_top), axis=1, keepdims=True)
    lo = n_groups + grp * epg
    e_logits = jnp.where(jnp.logical_and(lane >= lo, lane < lo + epg), logits, neg)
    v1, i1 = first_max(e_logits)
    v2, i2 = first_max(jnp.where(lane == i1, neg, e_logits))
    e21 = jnp.exp(v2 - v1)
    g1 = p_grp * (1.0 / (1.0 + e21))
    g2 = p_grp * (e21 / (1.0 + e21))
    eid = jnp.where(lane == 0.0, i1 - n_groups, jnp.where(lane == 1.0, i2 - n_groups, 0.0))
    eid_ref[...] = eid.astype(jnp.int32)
    gate_ref[...] = jnp.where(lane == 0.0, g1, jnp.where(lane == 1.0, g2, 0.0))


def _router(x, w_rg, b_rg, w_re, b_re):
    n, d = x.shape
    n_groups, epg = w_re.shape[1], w_re.shape[2]
    n_log = n_groups + n_groups * epg
    assert n_log <= LANES
    w = jnp.concatenate([w_rg, w_re.reshape(d, n_groups * epg), jnp.zeros((d, LANES - n_log), F32)], axis=1)
    b = jnp.concatenate([b_rg, b_re.reshape(-1), jnp.zeros((LANES - n_log,), F32)]).reshape(1, LANES)
    tm = _pick(n, (640, 512, 256, 128))
    row = pl.BlockSpec((tm, LANES), lambda i: (i, 0))
    eid, gate = pl.pallas_call(
        functools.partial(_router_kernel, n_groups=n_groups, epg=epg),
        grid=(n // tm,),
        in_specs=[pl.BlockSpec((tm, d), lambda i: (i, 0)),
                  pl.BlockSpec((d, LANES), lambda i: (0, 0)),
                  pl.BlockSpec((1, LANES), lambda i: (0, 0))],
        out_specs=[row, row],
        out_shape=[jax.ShapeDtypeStruct((n, LANES), jnp.int32), jax.ShapeDtypeStruct((n, LANES), F32)],
        compiler_params=_params("parallel"),
        name="router",
    )(x, w, b)
    return eid[:, :2], gate


def _gather_rows_kernel(tok_ref, cnt_ref, x_hbm, o_ref, sem):
    b = pl.program_id(0)
    base = b * EXPERT_BLOCK
    cnt = cnt_ref[b]

    def row_copy(r, src_row):
        return pltpu.make_async_copy(x_hbm.at[pl.ds(src_row, 1), :], o_ref.at[pl.ds(r, 1), :], sem)

    def start_pair(i, carry):
        row_copy(2 * i, tok_ref[base + 2 * i]).start(priority=0)
        row_copy(2 * i + 1, tok_ref[base + 2 * i + 1]).start(priority=1)
        return carry

    def wait(r, carry):
        row_copy(r, 0).wait()
        return carry

    def zero_row(r, carry):
        o_ref[pl.ds(r, 1), :] = jnp.zeros((1, o_ref.shape[1]), o_ref.dtype)
        return carry

    lax.fori_loop(0, jnp.right_shift(cnt, 1), start_pair, 0)

    @pl.when(jnp.bitwise_and(cnt, 1) == 1)
    def _():
        row_copy(cnt - 1, tok_ref[base + cnt - 1]).start()

    lax.fori_loop(cnt, EXPERT_BLOCK, zero_row, 0)
    lax.fori_loop(0, cnt, wait, 0)


def _gather_rows(x, row_tok, blk_cnt):
    n_rows = row_tok.shape[0]
    d = x.shape[1]
    return pl.pallas_call(
        _gather_rows_kernel,
        grid_spec=pltpu.PrefetchScalarGridSpec(
            num_scalar_prefetch=2,
            grid=(n_rows // EXPERT_BLOCK,),
            in_specs=[pl.BlockSpec(memory_space=pl.ANY)],
            out_specs=pl.BlockSpec((EXPERT_BLOCK, d), lambda b, tok, cnt: (b, 0)),
            scratch_shapes=[pltpu.SemaphoreType.DMA(())]),
        out_shape=jax.ShapeDtypeStruct((n_rows, d), x.dtype),
        compiler_params=_params("arbitrary"),
        name="gather_rows",
    )(row_tok, blk_cnt, x)


BLOCK_VALID = 1
BLOCK_NEW_EXPERT = 2


def _stream_expert_weights(plan, w_hbms, wbuf_ref, wbf_ref, sem, tile):
    be_ref, fl_ref, nx_ref, seg_ref, nseg_ref = plan
    p = pl.program_id(0)
    b = pl.program_id(1)
    n_seg = nseg_ref[0]
    g = p * n_seg + seg_ref[b]
    slot = jnp.bitwise_and(g, 1)

    def tile_copies(expert, col_pass, dst_slot):
        cols = pl.ds(pl.multiple_of(col_pass * tile, tile), tile)
        return [pltpu.make_async_copy(w.at[expert, :, cols], wbuf_ref.at[dst_slot, k], sem.at[dst_slot, k])
                for k, w in enumerate(w_hbms)]

    @pl.when(fl_ref[b] >= BLOCK_NEW_EXPERT)
    def _():
        @pl.when(g == 0)
        def _():
            for cp in tile_copies(be_ref[b], p, slot):
                cp.start()

        for cp in tile_copies(be_ref[b], p, slot):
            cp.wait()

        @pl.when(g + 1 < pl.num_programs(0) * n_seg)
        def _():
            next_pass = jnp.where(seg_ref[b] + 1 == n_seg, p + 1, p)
            for cp in tile_copies(nx_ref[b], next_pass, 1 - slot):
                cp.start()

        for k in range(len(w_hbms)):
            wbf_ref[k] = wbuf_ref[slot, k].astype(BF16)


def _expert_up_kernel(be_ref, fl_ref, nx_ref, seg_ref, nseg_ref, x_ref, wg_hbm, wu_hbm, o_ref, wbuf_ref, wbf_ref, sem):
    _stream_expert_weights((be_ref, fl_ref, nx_ref, seg_ref, nseg_ref), (wg_hbm, wu_hbm), wbuf_ref, wbf_ref, sem,
                           o_ref.shape[1])
    valid = jnp.bitwise_and(fl_ref[pl.program_id(1)], BLOCK_VALID) != 0

    @pl.when(valid)
    def _():
        x_hi, x_lo = _unpack_bf16_pairs(x_ref[...])
        half = x_hi.shape[1]
        dot = functools.partial(jnp.dot, preferred_element_type=F32)
        g = dot(x_hi, wbf_ref[0, 0:half, :]) + dot(x_lo, wbf_ref[0, half:2 * half, :])
        u = dot(x_hi, wbf_ref[1, 0:half, :]) + dot(x_lo, wbf_ref[1, half:2 * half, :])
        o_ref[...] = (_silu(g) * u).astype(o_ref.dtype)

    @pl.when(jnp.logical_not(valid))
    def _():
        o_ref[...] = jnp.zeros(o_ref.shape, o_ref.dtype)


def _expert_up(xs, plan, w_gate, w_up):
    n_rows = xs.shape[0]
    d, ff = w_gate.shape[1], w_gate.shape[2]
    tf = _pick(ff, (512, 256, 128))
    return pl.pallas_call(
        _expert_up_kernel,
        grid_spec=pltpu.PrefetchScalarGridSpec(
            num_scalar_prefetch=5,
            grid=(ff // tf, n_rows // EXPERT_BLOCK),
            in_specs=[pl.BlockSpec((EXPERT_BLOCK, d // 2), lambda f, b, *_: (b, 0)),
                      pl.BlockSpec(memory_space=pl.ANY), pl.BlockSpec(memory_space=pl.ANY)],
            out_specs=pl.BlockSpec((EXPERT_BLOCK, tf), lambda f, b, *_: (b, f)),
            scratch_shapes=[pltpu.VMEM((2, 2, d, tf), F32), pltpu.VMEM((2, d, tf), BF16),
                            pltpu.SemaphoreType.DMA((2, 2))]),
        out_shape=jax.ShapeDtypeStruct((n_rows, ff), BF16),
        compiler_params=_params("arbitrary", "arbitrary"),
        name="expert_up",
    )(*plan, xs, w_gate, w_up)


def _expert_down_kernel(be_ref, fl_ref, nx_ref, seg_ref, nseg_ref, h_ref, wd_hbm, o_ref, wbuf_ref, wbf_ref, sem):
    _stream_expert_weights((be_ref, fl_ref, nx_ref, seg_ref, nseg_ref), (wd_hbm,), wbuf_ref, wbf_ref, sem,
                           o_ref.shape[1])
    valid = jnp.bitwise_and(fl_ref[pl.program_id(1)], BLOCK_VALID) != 0

    @pl.when(valid)
    def _():
        o_ref[...] = jnp.dot(h_ref[...], wbf_ref[0], preferred_element_type=F32)

    @pl.when(jnp.logical_not(valid))
    def _():
        o_ref[...] = jnp.zeros(o_ref.shape, o_ref.dtype)


def _expert_down(hb, plan, w_down):
    n_rows, ff = hb.shape
    d = w_down.shape[2]
    tn = _pick(d, (2048, 1024, 512, 256, 128))
    return pl.pallas_call(
        _expert_down_kernel,
        grid_spec=pltpu.PrefetchScalarGridSpec(
            num_scalar_prefetch=5,
            grid=(d // tn, n_rows // EXPERT_BLOCK),
            in_specs=[pl.BlockSpec((EXPERT_BLOCK, ff), lambda n, b, *_: (b, 0)),
                      pl.BlockSpec(memory_space=pl.ANY)],
            out_specs=pl.BlockSpec((EXPERT_BLOCK, tn), lambda n, b, *_: (b, n)),
            scratch_shapes=[pltpu.VMEM((2, 1, ff, tn), F32), pltpu.VMEM((1, ff, tn), BF16),
                            pltpu.SemaphoreType.DMA((2, 1))]),
        out_shape=jax.ShapeDtypeStruct((n_rows, d), F32),
        compiler_params=_params("arbitrary", "arbitrary"),
        name="expert_down",
    )(*plan, hb, w_down)


def _combine_ln_kernel(d0_ref, d1_ref, ys_hbm, h_ref, gate_ref, g_ref, b_ref, op_ref, os_ref, buf_ref, sem,
                       *, alpha, prompt_blocks):
    tm = h_ref.shape[0]
    i = pl.program_id(0)
    base = i * tm

    def row_copy(k, r, src_row):
        return pltpu.make_async_copy(ys_hbm.at[pl.ds(src_row, 1), :], buf_ref.at[k, pl.ds(r, 1), :], sem)

    def start(r, carry):
        row_copy(0, r, d0_ref[base + r]).start()
        row_copy(1, r, d1_ref[base + r]).start()
        return carry

    def wait(r, carry):
        row_copy(0, r, 0).wait()
        row_copy(1, r, 0).wait()
        return carry

    lax.fori_loop(0, tm, start, 0)
    lax.fori_loop(0, tm, wait, 0)
    ffn = buf_ref[0] * gate_ref[:, 0:1] + buf_ref[1] * gate_ref[:, 1:2]
    y = _ln_rows(alpha * h_ref[...] + ffn, g_ref[...], b_ref[...])

    @pl.when(i < prompt_blocks)
    def _():
        op_ref[...] = y

    @pl.when(i >= prompt_blocks)
    def _():
        os_ref[...] = y


def _combine_ln(ys, dest0, dest1, h, gate, g, b, alpha, n_prompt):
    n, d = h.shape
    tm = _pick(math.gcd(n_prompt, n - n_prompt), (128, 64, 8))
    pb = n_prompt // tm
    row = pl.BlockSpec((tm, d), lambda i, d0, d1: (i, 0))
    vec = pl.BlockSpec((1, d), lambda i, d0, d1: (0, 0))
    return pl.pallas_call(
        functools.partial(_combine_ln_kernel, alpha=alpha, prompt_blocks=pb),
        grid_spec=pltpu.PrefetchScalarGridSpec(
            num_scalar_prefetch=2,
            grid=(n // tm,),
            in_specs=[pl.BlockSpec(memory_space=pl.ANY), row,
                      pl.BlockSpec((tm, LANES), lambda i, d0, d1: (i, 0)), vec, vec],
            out_specs=[pl.BlockSpec((tm, d), lambda i, d0, d1: (jnp.minimum(i, pb - 1), 0)),
                       pl.BlockSpec((tm, d), lambda i, d0, d1: (jnp.maximum(i - pb, 0), 0))],
            scratch_shapes=[pltpu.VMEM((2, tm, d), F32), pltpu.SemaphoreType.DMA(())]),
        out_shape=[jax.ShapeDtypeStruct((n_prompt, d), F32), jax.ShapeDtypeStruct((n - n_prompt, d), F32)],
        compiler_params=_params("arbitrary"),
        name="combine_ln",
    )(dest0, dest1, ys, h, gate, g.reshape(1, d), b.reshape(1, d))


def _dispatch_plan(eid, n_experts):
    n, k = eid.shape
    flat_e = eid.reshape(-1)
    onehot = (flat_e[:, None] == jnp.arange(n_experts, dtype=jnp.int32)[None, :]).astype(jnp.int32)
    csum = jnp.cumsum(onehot, axis=0)
    rank = jnp.sum(csum * onehot, axis=1) - 1
    counts = csum[-1]
    padded = (counts + EXPERT_BLOCK - 1) // EXPERT_BLOCK * EXPERT_BLOCK
    p_end = jnp.cumsum(padded)
    p_start = p_end - padded
    dest = (p_start[flat_e] + rank).astype(jnp.int32)
    n_blocks = -(-(n * k + n_experts * (EXPERT_BLOCK - 1)) // EXPERT_BLOCK)
    flat_tok = jnp.repeat(jnp.arange(n, dtype=jnp.int32), k)
    row_tok = jnp.zeros((n_blocks * EXPERT_BLOCK,), jnp.int32).at[dest].set(flat_tok, unique_indices=True)
    blk_row = jnp.arange(n_blocks, dtype=jnp.int32) * EXPERT_BLOCK
    blk_e = jnp.minimum(jnp.searchsorted(p_end, blk_row, side='right'), n_experts - 1).astype(jnp.int32)
    valid = blk_row < p_end[-1]
    blk_e = jnp.where(valid, blk_e, blk_e[p_end[-1] // EXPERT_BLOCK - 1])
    blk_cnt = jnp.clip(counts[blk_e] - (blk_row - p_start[blk_e]), 0, EXPERT_BLOCK).astype(jnp.int32)
    new_e = jnp.concatenate([jnp.ones((1,), bool), blk_e[1:] != blk_e[:-1]])
    flags = (valid.astype(jnp.int32) * BLOCK_VALID + new_e.astype(jnp.int32) * BLOCK_NEW_EXPERT)
    seg = jnp.cumsum(new_e.astype(jnp.int32)) - 1
    n_seg = seg[-1:] + 1
    seg_e = jnp.zeros((n_experts,), jnp.int32).at[seg].set(blk_e)
    next_e = seg_e[(seg + 1) % n_seg]
    plan = (blk_e, flags, next_e.astype(jnp.int32), seg.astype(jnp.int32), n_seg.astype(jnp.int32))
    return row_tok, blk_cnt, plan, dest.reshape(n, k)


def _pad_rows(a, rows):
    return jnp.pad(a, ((0, 0), (0, rows - a.shape[1]), (0, 0)))


def _layer(x_p, x_s, mem_prompt, win_k, win_v, conv_state, mem_k_s, mem_v_s,
           w_in, conv_w, conv_b, conv_ln_g, conv_ln_b, w_out, ln1_g, ln1_b,
           w_mem_q, w_mem_k, w_mem_v, w_mem_o, ln2_g, ln2_b,
           w_rg, b_rg, w_re, b_re, w_gate, w_up, w_down, ln3_g, ln3_b, alpha):
    bp, t, d = x_p.shape
    bs, ts, _ = x_s.shape
    past, heads = win_k.shape[1], win_k.shape[2]
    width = heads * HEAD_DIM
    conv_ch = conv_state.shape[2]
    cs = conv_ch // LANES
    mem_tokens, mem_heads = mem_k_s.shape[1], mem_k_s.shape[2]
    n_experts = w_gate.shape[0]
    assert bp == 1 and t % W_MAX == 0 and past == W_MAX and win_k.shape[3] == HEAD_DIM
    assert ts <= SAMPLE_U_PAD and w_in.shape[1] == 3 * width + 2 * conv_ch and width + conv_ch == d
    n_s = bs * ts

    x = jnp.concatenate([x_p.reshape(t, d), x_s.reshape(n_s, d)], axis=0)
    proj3 = _matmul_slabs(x, w_in.astype(BF16))

    def sample_part(lo, hi):
        return jnp.transpose(proj3[lo:hi, t:], (1, 0, 2)).reshape(bs, ts, (hi - lo) * LANES)

    q_s = sample_part(0, heads)
    k_s = sample_part(heads, 2 * heads)
    v_s = sample_part(2 * heads, 3 * heads)
    ga_s = sample_part(3 * heads, 3 * heads + cs)
    gb_s = sample_part(3 * heads + cs, 3 * heads + 2 * cs)

    put_sample = lambda full, rows: lax.dynamic_update_slice(full, rows.astype(full.dtype), (t,) + (0,) * (full.ndim - 1))
    attn_s, win_k_s, win_v_s = _attn_sample(_pad_rows(q_s, SAMPLE_Q_PAD), k_s.reshape(bs, ts, heads, HEAD_DIM),
                                            v_s.reshape(bs, ts, heads, HEAD_DIM), win_k, win_v)
    attn = put_sample(_attn_prompt(proj3, t, heads), attn_s[:, :ts].reshape(n_s, width))
    keep_p = min(W_MAX, t)
    win_k_p = jnp.transpose(proj3[heads:2 * heads, t - keep_p:t], (1, 0, 2))[None]
    win_v_p = jnp.transpose(proj3[2 * heads:3 * heads, t - keep_p:t], (1, 0, 2))[None]

    conv_p, u_last = _conv_prompt(proj3, t, 3 * heads, 3 * heads + cs, cs, conv_w, conv_b)
    conv_state_p = u_last[CONV_HALO - (CONV_W - 1):][None]
    conv_s, conv_state_s = _conv_sample(conv_state, _pad_rows(ga_s, SAMPLE_U_PAD), _pad_rows(gb_s, SAMPLE_U_PAD),
                                        conv_w, conv_b, ts)
    conv_all = put_sample(conv_p, conv_s[:, :ts].reshape(n_s, conv_ch))
    c_all = _ln_silu(conv_all, conv_ln_g, conv_ln_b)

    h1, h1_bf = _matmul_ln([attn, c_all], w_out.astype(BF16), x, ln1_g, ln1_b, alpha)

    qm = _matmul(h1_bf, w_mem_q.astype(BF16), out_dtype=BF16)
    mem_x = mem_prompt.reshape(bp * mem_tokens, d)
    mem_k_p = _matmul(mem_x, w_mem_k.astype(BF16))
    mem_v_p = _matmul(mem_x, w_mem_v.astype(BF16))
    om_s = _mem_attn(_pad_rows(qm[t:].reshape(bs, ts, d), SAMPLE_Q_PAD), mem_k_s.reshape(bs, mem_tokens, d),
                     mem_v_s.reshape(bs, mem_tokens, d), mem_heads)[:, :ts]
    om = put_sample(_mem_attn(qm[None], mem_k_p[None], mem_v_p[None], mem_heads)[0], om_s.reshape(n_s, d))
    h2, h2_packed = _matmul_ln([om], w_mem_o.astype(BF16), h1, ln2_g, ln2_b, alpha, pack=True)

    eid, gate = _router(h2, w_rg, b_rg, w_re, b_re)
    row_tok, blk_cnt, plan, dest = _dispatch_plan(eid, n_experts)
    xs = _gather_rows(h2_packed, row_tok, blk_cnt)
    hb = _expert_up(xs, plan, w_gate, w_up)
    ys = _expert_down(hb, plan, w_down)
    y_p, y_s = _combine_ln(ys, dest[:, 0], dest[:, 1], h2, gate, ln3_g, ln3_b, alpha, t)

    mem_shape = (bp, mem_tokens, mem_heads, d // mem_heads)
    return (y_p.reshape(bp, t, d), y_s.reshape(bs, ts, d), win_k_p, win_v_p, conv_state_p,
            mem_k_p.reshape(mem_shape), mem_v_p.reshape(mem_shape), win_k_s, win_v_s, conv_state_s)


def kernel(x_prompt, x_sample, mem_prompt, cache_win_k, cache_win_v, state_conv, cache_mem_k, cache_mem_v, w_in, conv_w, conv_b, conv_ln_g, conv_ln_b, w_out, ln1_g, ln1_b, w_mem_q, w_mem_k, w_mem_v, w_mem_o, ln2_g, ln2_b, w_router_group, b_router_group, w_router_expert, b_router_expert, w_exp_gate, w_exp_up, w_exp_down, ln3_g, ln3_b):
    depth = w_in.shape[0]
    alpha = (2 * depth) ** 0.25
    hp, hs = x_prompt, x_sample
    per_layer = []
    for l in range(depth):
        outs = _layer(hp, hs, mem_prompt, cache_win_k[l], cache_win_v[l], state_conv[l], cache_mem_k[l], cache_mem_v[l],
                      w_in[l], conv_w[l], conv_b[l], conv_ln_g[l], conv_ln_b[l], w_out[l], ln1_g[l], ln1_b[l],
                      w_mem_q[l], w_mem_k[l], w_mem_v[l], w_mem_o[l], ln2_g[l], ln2_b[l],
                      w_router_group[l], b_router_group[l], w_router_expert[l], b_router_expert[l],
                      w_exp_gate[l], w_exp_up[l], w_exp_down[l], ln3_g[l], ln3_b[l], alpha)
        hp, hs = outs[0], outs[1]
        per_layer.append(outs[2:])
    stacked = [jnp.stack([layer[i] for layer in per_layer]) for i in range(8)]
    return (hp, hs, *stacked)
```

```python
import functools
import math

import jax
import jax.numpy as jnp
from jax import lax
from jax.experimental import pallas as pl
from jax.experimental.pallas import tpu as pltpu

F32 = jnp.float32
BF16 = jnp.bfloat16

LANES = 128
HEAD_DIM = 128
DILATED_BRANCHES = ((128, 1), (512, 4), (2048, 16))
W_MAX = 2048
KEYS_PER_BLOCK = 128
ATTN_GROUP = 8
CONV_W = 31
CONV_HALO = 32
EXPERT_BLOCK = 128
LN_EPS = 1e-5
VMEM_LIMIT = 56 * 1024 * 1024


def _params(*sem):
    return pltpu.CompilerParams(dimension_semantics=sem, vmem_limit_bytes=VMEM_LIMIT)


def _pick(n, candidates):
    for c in candidates:
        if n % c == 0:
            return c
    return n


def _mm_kernel(*refs):
    a_refs, b_ref, o_ref = refs[:-2], refs[-2], refs[-1]
    acc = None
    k0 = 0
    for a_ref in a_refs:
        k1 = k0 + a_ref.shape[1]
        part = jnp.dot(a_ref[...].astype(BF16), b_ref[k0:k1, :].astype(BF16), preferred_element_type=F32)
        acc = part if acc is None else acc + part
        k0 = k1
    o_ref[...] = acc.astype(o_ref.dtype)


def _matmul(srcs, b, out_dtype=F32):
    m = srcs[0].shape[0]
    k, n = b.shape
    assert sum(a.shape[1] for a in srcs) == k
    tm = _pick(m, (640, 512, 256, 128))
    tn = _pick(n, (512, 256, 128))
    return pl.pallas_call(
        _mm_kernel,
        grid=(m // tm, n // tn),
        in_specs=[pl.BlockSpec((tm, a.shape[1]), lambda i, j: (i, 0)) for a in srcs]
        + [pl.BlockSpec((k, tn), lambda i, j: (0, j))],
        out_specs=pl.BlockSpec((tm, tn), lambda i, j: (i, j)),
        out_shape=jax.ShapeDtypeStruct((m, n), out_dtype),
        compiler_params=_params("parallel", "arbitrary"),
        name="matmul",
    )(*srcs, b)


def _mm_slab_kernel(a_ref, b_ref, o_ref):
    r = jnp.dot(a_ref[...].astype(BF16), b_ref[...].astype(BF16), preferred_element_type=F32)
    for s in range(o_ref.shape[0]):
        o_ref[s] = r[:, s * LANES:(s + 1) * LANES]


def _matmul_slabs(a, b):
    m, k = a.shape
    n = b.shape[1]
    tm = _pick(m, (640, 512, 256, 128))
    tn = _pick(n, (512, 256, 128))
    return pl.pallas_call(
        _mm_slab_kernel,
        grid=(m // tm, n // tn),
        in_specs=[pl.BlockSpec((tm, k), lambda i, j: (i, 0)),
                  pl.BlockSpec((k, tn), lambda i, j: (0, j))],
        out_specs=pl.BlockSpec((tn // LANES, tm, LANES), lambda i, j: (j, i, 0)),
        out_shape=jax.ShapeDtypeStruct((n // LANES, m, LANES), F32),
        compiler_params=_params("parallel", "arbitrary"),
        name="matmul_slabs",
    )(a, b)


def _ln_rows(x, g, b):
    mu = jnp.mean(x, axis=-1, keepdims=True)
    xc = x - mu
    var = jnp.mean(xc * xc, axis=-1, keepdims=True)
    return xc * lax.rsqrt(var + LN_EPS) * g + b


def _silu(x):
    return x * jax.nn.sigmoid(x)


def _pack_bf16_pairs(hi, lo):
    hi = lax.bitcast_convert_type(hi.astype(BF16).astype(F32), jnp.uint32)
    lo = lax.bitcast_convert_type(lo.astype(BF16).astype(F32), jnp.uint32)
    return jnp.bitwise_or(hi, jnp.right_shift(lo, jnp.uint32(16)))


def _unpack_bf16_pairs(u):
    hi = lax.bitcast_convert_type(jnp.bitwise_and(u, jnp.uint32(0xFFFF0000)), F32)
    lo = lax.bitcast_convert_type(jnp.left_shift(u, jnp.uint32(16)), F32)
    return hi.astype(BF16), lo.astype(BF16)


def _res_ln_kernel(res_ref, t_ref, g_ref, b_ref, o_ref, o2_ref, *, alpha, pack):
    y = _ln_rows(alpha * res_ref[...] + t_ref[...], g_ref[...], b_ref[...])
    o_ref[...] = y
    half = y.shape[1] // 2
    o2_ref[...] = _pack_bf16_pairs(y[:, :half], y[:, half:]) if pack else y.astype(BF16)


def _res_ln(res, t, g, b, alpha, pack=False):
    n, d = res.shape
    tm = _pick(n, (128, 64, 8))
    row = pl.BlockSpec((tm, d), lambda i: (i, 0))
    vec = pl.BlockSpec((1, d), lambda i: (0, 0))
    second = (jax.ShapeDtypeStruct((n, d // 2), jnp.uint32), pl.BlockSpec((tm, d // 2), lambda i: (i, 0))) if pack \
        else (jax.ShapeDtypeStruct((n, d), BF16), row)
    return pl.pallas_call(
        functools.partial(_res_ln_kernel, alpha=alpha, pack=pack),
        grid=(n // tm,),
        in_specs=[row, row, vec, vec],
        out_specs=[row, second[1]],
        out_shape=[jax.ShapeDtypeStruct((n, d), F32), second[0]],
        compiler_params=_params("parallel"),
        name="res_ln",
    )(res, t, g.reshape(1, d), b.reshape(1, d))


def _ln_silu_kernel(x_ref, g_ref, b_ref, o_ref):
    o_ref[...] = _silu(_ln_rows(x_ref[...], g_ref[...], b_ref[...])).astype(o_ref.dtype)


def _ln_silu(x, g, b):
    n, d = x.shape
    tm = _pick(n, (256, 128, 64, 8))
    row = pl.BlockSpec((tm, d), lambda i: (i, 0))
    vec = pl.BlockSpec((1, d), lambda i: (0, 0))
    return pl.pallas_call(
        _ln_silu_kernel,
        grid=(n // tm,),
        in_specs=[row, vec, vec],
        out_specs=row,
        out_shape=jax.ShapeDtypeStruct((n, d), BF16),
        compiler_params=_params("parallel"),
        name="ln_silu",
    )(x, g.reshape(1, d), b.reshape(1, d))


def _attn_prompt_body(c, n_chunks, q_ref, kp_ref, kc_ref, vp_ref, vc_ref, o_ref, acc_ref, m_ref, l_ref):
    scale = 1.0 / math.sqrt(HEAD_DIM)
    blk = KEYS_PER_BLOCK
    ii = lax.broadcasted_iota(jnp.int32, (blk, blk), 0)
    jj = lax.broadcasted_iota(jnp.int32, (blk, blk), 1)
    nt = (((1,), (1,)), ((), ()))

    def rows(start, d):
        return pl.ds(start, blk) if d == 1 else pl.ds(start, blk, stride=d)

    def sub_blocks(subs, d, first, last):
        n = range(len(subs))
        cur = [rows(r + d * blk * b, d) for r, b in subs]
        prev = [rows(r + W_MAX - d * blk, d) if b == 0 else rows(r + d * blk * (b - 1), d) for r, b in subs]
        k_prev = [kp_ref if b == 0 else kc_ref for _, b in subs]
        v_prev = [vp_ref if b == 0 else vc_ref for _, b in subs]
        q = [(q_ref[cur[i], :] * scale).astype(BF16) for i in n]
        s_a = [lax.dot_general(q[i], k_prev[i][prev[i], :].astype(BF16), nt, preferred_element_type=F32) for i in n]
        s_b = [lax.dot_general(q[i], kc_ref[cur[i], :].astype(BF16), nt, preferred_element_type=F32) for i in n]
        for i, (_, b) in enumerate(subs):
            mask_a = jj >= ii
            if b == 0:
                mask_a = jnp.logical_and(mask_a, c > 0)
            s_a[i] = jnp.where(mask_a, s_a[i], -jnp.inf)
            s_b[i] = jnp.where(jj <= ii, s_b[i], -jnp.inf)
        m_loc = [jnp.maximum(jnp.max(s_a[i], axis=1, keepdims=True), jnp.max(s_b[i], axis=1, keepdims=True))
                 for i in n]
        p_a = [jnp.exp(s_a[i] - m_loc[i]) for i in n]
        p_b = [jnp.exp(s_b[i] - m_loc[i]) for i in n]
        l_loc = [jnp.sum(p_a[i], axis=1, keepdims=True) + jnp.sum(p_b[i], axis=1, keepdims=True) for i in n]
        acc_loc = [jnp.dot(p_a[i].astype(BF16), v_prev[i][prev[i], :].astype(BF16), preferred_element_type=F32)
                   + jnp.dot(p_b[i].astype(BF16), vc_ref[cur[i], :].astype(BF16), preferred_element_type=F32)
                   for i in n]
        for i in n:
            if first:
                m_new, l_new, acc_new = m_loc[i], l_loc[i], acc_loc[i]
            else:
                m_old = m_ref[cur[i], :][:, :1]
                l_old = l_ref[cur[i], :][:, :1]
                m_new = jnp.maximum(m_old, m_loc[i])
                a_old = jnp.exp(m_old - m_new)
                a_loc = jnp.exp(m_loc[i] - m_new)
                l_new = a_old * l_old + a_loc * l_loc[i]
                acc_new = a_old * acc_ref[cur[i], :] + a_loc * acc_loc[i]
            if last:
                o_ref[cur[i], :] = (acc_new / l_new).astype(o_ref.dtype)
            else:
                m_ref[cur[i], :] = jnp.broadcast_to(m_new, (blk, LANES))
                l_ref[cur[i], :] = jnp.broadcast_to(l_new, (blk, LANES))
                acc_ref[cur[i], :] = acc_new

    @pl.when(c < n_chunks)
    def _():
        order = sorted(DILATED_BRANCHES, key=lambda wd: -wd[1])
        for idx, (w, d) in enumerate(order):
            subs = [(r, b) for b in range(W_MAX // (d * blk)) for r in range(d)]
            for g in range(0, len(subs), ATTN_GROUP):
                sub_blocks(subs[g:g + ATTN_GROUP], d, idx == 0, idx == len(order) - 1)

    @pl.when(c >= n_chunks)
    def _():
        o_ref[...] = jnp.zeros(o_ref.shape, o_ref.dtype)


SAMPLE_Q_PAD = 8
SAMPLE_HEADS = 8


def _key_multiplicity(dist):
    mult = jnp.zeros(dist.shape, F32)
    for (w, d) in DILATED_BRANCHES:
        hit = jnp.logical_and(jnp.bitwise_and(dist, d - 1) == 0, dist <= w)
        mult = mult + jnp.where(hit, 1.0, 0.0)
    return jnp.where(dist >= 0, mult, 0.0)


def _attn_sample_body(b, g, q_ref, kn_ref, vn_ref, kc_ref, vc_ref, o_ref, wk_hbm, wv_hbm, sem):
    _, past, hg, _ = kc_ref.shape
    t_new = kn_ref.shape[1]
    heads = pl.ds(pl.multiple_of(g * hg, hg), hg)
    copies = []
    for k, (c_ref, n_ref, w_hbm) in enumerate(((kc_ref, kn_ref, wk_hbm), (vc_ref, vn_ref, wv_hbm))):
        copies.append(pltpu.make_async_copy(c_ref.at[0, pl.ds(t_new, past - t_new)],
                                            w_hbm.at[b, pl.ds(0, past - t_new), heads, :], sem.at[2 * k]))
        copies.append(pltpu.make_async_copy(n_ref.at[0], w_hbm.at[b, pl.ds(past - t_new, t_new), heads, :],
                                            sem.at[2 * k + 1]))
    for cp in copies:
        cp.start()

    scale = 1.0 / math.sqrt(HEAD_DIM)
    qi = lax.broadcasted_iota(jnp.int32, (SAMPLE_Q_PAD, past), 0)
    rho = lax.broadcasted_iota(jnp.int32, (SAMPLE_Q_PAD, past), 1)
    mult_c = _key_multiplicity(past + qi - rho)
    qi1 = lax.broadcasted_iota(jnp.int32, (SAMPLE_Q_PAD, 1), 0)
    mult_n = [_key_multiplicity(qi1 - i) for i in range(t_new)]
    nt = (((1,), (1,)), ((), ()))
    k_rows = kc_ref.at[0].reshape(past * hg, HEAD_DIM)
    v_rows = vc_ref.at[0].reshape(past * hg, HEAD_DIM)
    for h in range(hg):
        sl = slice(h * HEAD_DIM, (h + 1) * HEAD_DIM)
        head_rows = pl.ds(h, past, stride=hg)
        q = q_ref[0, :, sl] * scale
        s_c = lax.dot_general(q.astype(BF16), k_rows[head_rows, :].astype(BF16), nt, preferred_element_type=F32)
        s_c = jnp.where(mult_c > 0, s_c, -jnp.inf)
        s_n = [jnp.sum(q * kn_ref[0, i, h:h + 1, :], axis=1, keepdims=True) for i in range(t_new)]
        s_n = [jnp.where(mult_n[i] > 0, s_n[i], -jnp.inf) for i in range(t_new)]
        m = jnp.max(s_c, axis=1, keepdims=True)
        for i in range(t_new):
            m = jnp.maximum(m, s_n[i])
        p_c = mult_c * jnp.exp(s_c - m)
        l = jnp.sum(p_c, axis=1, keepdims=True)
        o = jnp.dot(p_c.astype(BF16), v_rows[head_rows, :].astype(BF16), preferred_element_type=F32)
        for i in range(t_new):
            p_n = mult_n[i] * jnp.exp(s_n[i] - m)
            l = l + p_n
            o = o + p_n * vn_ref[0, i, h:h + 1, :]
        o_ref[0, :, sl] = o / l

    for cp in copies:
        cp.wait()


def _attn_kernel(*refs, n_chunks, heads, prompt_steps, sample_steps, sample_groups):
    prompt_in, sample_in = refs[0:5], refs[5:10]
    o_ref, so_ref, wk_hbm, wv_hbm, acc_ref, m_ref, l_ref, sem = refs[10:]
    s = pl.program_id(0)

    @pl.when(s < sample_steps)
    def _():
        _attn_sample_body(s // sample_groups, s % sample_groups, *sample_in, so_ref, wk_hbm, wv_hbm, sem)

    @pl.when(s < prompt_steps)
    def _():
        _attn_prompt_body(s // heads, n_chunks, *prompt_in, o_ref, acc_ref, m_ref, l_ref)


def _attn(proj3, t, heads, q, k_new, v_new, k_cache, v_cache):
    n_total = proj3.shape[1]
    n_chunks = t // W_MAX
    bsz, past, _, _ = k_cache.shape
    t_new = k_new.shape[1]
    hg = SAMPLE_HEADS if heads % SAMPLE_HEADS == 0 else heads
    groups = heads // hg
    prompt_steps = pl.cdiv(n_total, W_MAX) * heads
    sample_steps = bsz * groups

    ps = lambda s: jnp.minimum(s, prompt_steps - 1)
    chunk = lambda s: jnp.minimum(ps(s) // heads, n_chunks - 1)
    blk = (None, W_MAX, LANES)
    cur = lambda off: pl.BlockSpec(blk, lambda s: (off + ps(s) % heads, chunk(s), 0))
    prev = lambda off: pl.BlockSpec(blk, lambda s: (off + ps(s) % heads, jnp.maximum(chunk(s) - 1, 0), 0))
    ss = lambda s: jnp.minimum(s, sample_steps - 1)
    qspec = pl.BlockSpec((1, SAMPLE_Q_PAD, hg * HEAD_DIM), lambda s: (ss(s) // groups, 0, ss(s) % groups))
    kv = lambda rows: pl.BlockSpec((1, rows, hg, HEAD_DIM), lambda s: (ss(s) // groups, 0, ss(s) % groups, 0))
    win = jax.ShapeDtypeStruct(k_cache.shape, F32)
    return pl.pallas_call(
        functools.partial(_attn_kernel, n_chunks=n_chunks, heads=heads, prompt_steps=prompt_steps,
                          sample_steps=sample_steps, sample_groups=groups),
        grid=(max(prompt_steps, sample_steps),),
        in_specs=[cur(0), prev(heads), cur(heads), prev(2 * heads), cur(2 * heads),
                  qspec, kv(t_new), kv(t_new), kv(past), kv(past)],
        out_specs=[pl.BlockSpec((W_MAX, LANES), lambda s: (ps(s) // heads, ps(s) % heads)), qspec,
                   pl.BlockSpec(memory_space=pl.ANY), pl.BlockSpec(memory_space=pl.ANY)],
        out_shape=[jax.ShapeDtypeStruct((n_total, heads * LANES), BF16),
                   jax.ShapeDtypeStruct((bsz, SAMPLE_Q_PAD, heads * HEAD_DIM), F32), win, win],
        scratch_shapes=[pltpu.VMEM((W_MAX, LANES), F32)] * 3 + [pltpu.SemaphoreType.DMA((4,))],
        compiler_params=_params("arbitrary"),
        name="attn",
    )(proj3, proj3, proj3, proj3, proj3, q, k_new, v_new, k_cache, v_cache)


CONV_ROWS = 64


def _conv_taps(ubuf_ref, w_ref, cb_ref, out_ref, n_rows, base):
    for r0 in range(0, n_rows, CONV_ROWS):
        nr = min(CONV_ROWS, n_rows - r0)
        acc = jnp.broadcast_to(cb_ref[...], (nr, LANES))
        for j in range(CONV_W):
            acc = acc + ubuf_ref[pl.ds(base + r0 + j, nr), :] * w_ref[j:j + 1, :]
        out_ref[pl.ds(r0, nr), :] = acc


def _conv_prompt_kernel(ga_ref, gb_ref, w_ref, cb_ref, o_ref, ulast_ref, ubuf_ref, *, n_blocks):
    i = pl.program_id(1)
    tb = ga_ref.shape[0]

    @pl.when(i == 0)
    def _():
        ubuf_ref[0:CONV_HALO, :] = jnp.zeros((CONV_HALO, LANES), F32)

    @pl.when(jnp.logical_and(i > 0, i < n_blocks))
    def _():
        ubuf_ref[0:CONV_HALO, :] = ubuf_ref[tb:tb + CONV_HALO, :]

    @pl.when(i < n_blocks)
    def _():
        ubuf_ref[CONV_HALO:CONV_HALO + tb, :] = ga_ref[...] * jax.nn.sigmoid(gb_ref[...])
        _conv_taps(ubuf_ref, w_ref, cb_ref, o_ref, tb, CONV_HALO - (CONV_W - 1))

    @pl.when(i == n_blocks - 1)
    def _():
        ulast_ref[...] = ubuf_ref[tb:tb + CONV_HALO, :]

    @pl.when(i >= n_blocks)
    def _():
        o_ref[...] = jnp.zeros(o_ref.shape, o_ref.dtype)


def _conv_prompt(proj3, t, slab_a, slab_b, n_slabs, conv_w, conv_b):
    n_total = proj3.shape[1]
    tb = _pick(t, (512, 256, 128))
    c = n_slabs * LANES
    n_blocks = t // tb
    blk = lambda i: jnp.minimum(i, n_blocks - 1)
    return pl.pallas_call(
        functools.partial(_conv_prompt_kernel, n_blocks=n_blocks),
        grid=(n_slabs, pl.cdiv(n_total, tb)),
        in_specs=[pl.BlockSpec((None, tb, LANES), lambda s, i: (slab_a + s, blk(i), 0)),
                  pl.BlockSpec((None, tb, LANES), lambda s, i: (slab_b + s, blk(i), 0)),
                  pl.BlockSpec((CONV_W, LANES), lambda s, i: (0, s)),
                  pl.BlockSpec((1, LANES), lambda s, i: (0, s))],
        out_specs=[pl.BlockSpec((tb, LANES), lambda s, i: (i, s)),
                   pl.BlockSpec((CONV_HALO, LANES), lambda s, i: (0, s))],
        out_shape=[jax.ShapeDtypeStruct((n_total, c), F32), jax.ShapeDtypeStruct((CONV_HALO, c), F32)],
        scratch_shapes=[pltpu.VMEM((CONV_HALO + tb, LANES), F32)],
        compiler_params=_params("parallel", "arbitrary"),
        name="conv_prompt",
    )(proj3, proj3, conv_w, conv_b.reshape(1, c))


SAMPLE_U_PAD = 8


def _conv_sample_kernel(st_ref, ga_ref, gb_ref, w_ref, cb_ref, o_ref, nst_ref, ubuf_ref, *, t_new):
    bsz, hist, _ = st_ref.shape

    def one_sequence(b, carry):
        ubuf_ref[0:hist, :] = st_ref[b]
        ubuf_ref[hist:hist + SAMPLE_U_PAD, :] = ga_ref[b] * jax.nn.sigmoid(gb_ref[b])
        _conv_taps(ubuf_ref, w_ref, cb_ref, o_ref.at[b], SAMPLE_U_PAD, 0)
        nst_ref[b] = ubuf_ref[pl.ds(t_new, hist), :]
        return carry

    lax.fori_loop(0, bsz, one_sequence, 0)


def _conv_sample(state, ga, gb, conv_w, conv_b, t_new):
    bsz, hist, c = state.shape
    assert hist == CONV_W - 1 and t_new <= SAMPLE_U_PAD
    spec = lambda rows: pl.BlockSpec((bsz, rows, LANES), lambda s: (0, 0, s))
    return pl.pallas_call(
        functools.partial(_conv_sample_kernel, t_new=t_new),
        grid=(c // LANES,),
        in_specs=[spec(hist), spec(SAMPLE_U_PAD), spec(SAMPLE_U_PAD),
                  pl.BlockSpec((CONV_W, LANES), lambda s: (0, s)),
                  pl.BlockSpec((1, LANES), lambda s: (0, s))],
        out_specs=[spec(SAMPLE_U_PAD), spec(hist)],
        out_shape=[jax.ShapeDtypeStruct((bsz, SAMPLE_U_PAD, c), F32),
                   jax.ShapeDtypeStruct((bsz, hist, c), F32)],
        scratch_shapes=[pltpu.VMEM((hist + SAMPLE_U_PAD + 2, LANES), F32)],
        compiler_params=_params("parallel"),
        name="conv_sample",
    )(state, ga, gb, conv_w, conv_b.reshape(1, c))


def _mem_attn_kernel(q_ref, k_ref, v_ref, o_ref):
    hd = q_ref.shape[-1]
    nt = (((1,), (1,)), ((), ()))
    s = lax.dot_general(q_ref[...].astype(BF16), k_ref[...].astype(BF16), nt,
                        preferred_element_type=F32) * (1.0 / math.sqrt(hd))
    m = jnp.max(s, axis=1, keepdims=True)
    p = jnp.exp(s - m)
    l = jnp.sum(p, axis=1, keepdims=True)
    o = jnp.dot(p.astype(BF16), v_ref[...].astype(BF16), preferred_element_type=F32)
    o_ref[...] = (o / l).astype(o_ref.dtype)


def _mem_attn(q, k, v, heads):
    bsz, tq_all, d = q.shape
    m = k.shape[1]
    hd = d // heads
    tq = 512 if tq_all >= 512 else tq_all
    qspec = pl.BlockSpec((None, tq, hd), lambda b, h, i: (b, i, h))
    kspec = pl.BlockSpec((None, m, hd), lambda b, h, i: (b, 0, h))
    return pl.pallas_call(
        _mem_attn_kernel,
        grid=(bsz, heads, pl.cdiv(tq_all, tq)),
        in_specs=[qspec, kspec, kspec],
        out_specs=qspec,
        out_shape=jax.ShapeDtypeStruct((bsz, tq_all, d), BF16),
        compiler_params=_params("parallel", "parallel", "arbitrary"),
        name="mem_attn",
    )(q, k, v)


def _router_kernel(x_ref, w_ref, b_ref, eid_ref, gate_ref, *, n_groups, epg):
    logits = jnp.dot(x_ref[...].astype(BF16), w_ref[...].astype(BF16), preferred_element_type=F32) + b_ref[...]
    lane = lax.broadcasted_iota(jnp.int32, logits.shape, 1).astype(F32)
    neg = -jnp.inf
    none = float(LANES)

    def first_max(vals):
        top = jnp.max(vals, axis=1, keepdims=True)
        idx = jnp.min(jnp.where(vals == top, lane, none), axis=1, keepdims=True)
        return top, idx

    g_logits = jnp.where(lane < n_groups, logits, neg)
    g_top, grp = first_max(g_logits)
    p_grp = 1.0 / jnp.sum(jnp.exp(g_logits - g_top), axis=1, keepdims=True)
    lo = n_groups + grp * epg
    e_logits = jnp.where(jnp.logical_and(lane >= lo, lane < lo + epg), logits, neg)
    v1, i1 = first_max(e_logits)
    v2, i2 = first_max(jnp.where(lane == i1, neg, e_logits))
    e21 = jnp.exp(v2 - v1)
    g1 = p_grp * (1.0 / (1.0 + e21))
    g2 = p_grp * (e21 / (1.0 + e21))
    eid = jnp.where(lane == 0.0, i1 - n_groups, jnp.where(lane == 1.0, i2 - n_groups, 0.0))
    eid_ref[...] = eid.astype(jnp.int32)
    gate_ref[...] = jnp.where(lane == 0.0, g1, jnp.where(lane == 1.0, g2, 0.0))


def _router(x, w_rg, b_rg, w_re, b_re):
    n, d = x.shape
    n_groups, epg = w_re.shape[1], w_re.shape[2]
    n_log = n_groups + n_groups * epg
    assert n_log <= LANES
    w = jnp.concatenate([w_rg, w_re.reshape(d, n_groups * epg), jnp.zeros((d, LANES - n_log), F32)], axis=1)
    b = jnp.concatenate([b_rg, b_re.reshape(-1), jnp.zeros((LANES - n_log,), F32)]).reshape(1, LANES)
    tm = _pick(n, (640, 512, 256, 128))
    row = pl.BlockSpec((tm, LANES), lambda i: (i, 0))
    eid, gate = pl.pallas_call(
        functools.partial(_router_kernel, n_groups=n_groups, epg=epg),
        grid=(n // tm,),
        in_specs=[pl.BlockSpec((tm, d), lambda i: (i, 0)),
                  pl.BlockSpec((d, LANES), lambda i: (0, 0)),
                  pl.BlockSpec((1, LANES), lambda i: (0, 0))],
        out_specs=[row, row],
        out_shape=[jax.ShapeDtypeStruct((n, LANES), jnp.int32), jax.ShapeDtypeStruct((n, LANES), F32)],
        compiler_params=_params("parallel"),
        name="router",
    )(x, w, b)
    return eid[:, :2], gate


def _gather_rows_kernel(tok_ref, cnt_ref, x_hbm, o_ref, sem):
    b = pl.program_id(0)
    base = b * EXPERT_BLOCK
    cnt = cnt_ref[b]

    def row_copy(r, src_row):
        return pltpu.make_async_copy(x_hbm.at[pl.ds(src_row, 1), :], o_ref.at[pl.ds(r, 1), :], sem)

    def start_pair(i, carry):
        row_copy(2 * i, tok_ref[base + 2 * i]).start(priority=0)
        row_copy(2 * i + 1, tok_ref[base + 2 * i + 1]).start(priority=1)
        return carry

    def wait(r, carry):
        row_copy(r, 0).wait()
        return carry

    def zero_row(r, carry):
        o_ref[pl.ds(r, 1), :] = jnp.zeros((1, o_ref.shape[1]), o_ref.dtype)
        return carry

    lax.fori_loop(0, jnp.right_shift(cnt, 1), start_pair, 0)

    @pl.when(jnp.bitwise_and(cnt, 1) == 1)
    def _():
        row_copy(cnt - 1, tok_ref[base + cnt - 1]).start()

    lax.fori_loop(cnt, EXPERT_BLOCK, zero_row, 0)
    lax.fori_loop(0, cnt, wait, 0)


def _gather_rows(x, row_tok, blk_cnt):
    n_rows = row_tok.shape[0]
    d = x.shape[1]
    return pl.pallas_call(
        _gather_rows_kernel,
        grid_spec=pltpu.PrefetchScalarGridSpec(
            num_scalar_prefetch=2,
            grid=(n_rows // EXPERT_BLOCK,),
            in_specs=[pl.BlockSpec(memory_space=pl.ANY)],
            out_specs=pl.BlockSpec((EXPERT_BLOCK, d), lambda b, tok, cnt: (b, 0)),
            scratch_shapes=[pltpu.SemaphoreType.DMA(())]),
        out_shape=jax.ShapeDtypeStruct((n_rows, d), x.dtype),
        compiler_params=_params("arbitrary"),
        name="gather_rows",
    )(row_tok, blk_cnt, x)


BLOCK_VALID = 1
BLOCK_NEW_EXPERT = 2
WEIGHT_DMA_PRIORITY = 1


def _stream_expert_weights(plan, w_hbms, wbuf_ref, wbf_ref, sem, tile):
    be_ref, fl_ref, nx_ref, seg_ref, nseg_ref = plan
    p = pl.program_id(0)
    b = pl.program_id(1)
    n_seg = nseg_ref[0]
    g = p * n_seg + seg_ref[b]
    slot = jnp.bitwise_and(g, 1)

    def tile_copies(expert, col_pass, dst_slot):
        cols = pl.ds(pl.multiple_of(col_pass * tile, tile), tile)
        return [pltpu.make_async_copy(w.at[expert, :, cols], wbuf_ref.at[dst_slot, k], sem.at[dst_slot, k])
                for k, w in enumerate(w_hbms)]

    @pl.when(fl_ref[b] >= BLOCK_NEW_EXPERT)
    def _():
        @pl.when(g == 0)
        def _():
            for cp in tile_copies(be_ref[b], p, slot):
                cp.start(priority=WEIGHT_DMA_PRIORITY)

        for cp in tile_copies(be_ref[b], p, slot):
            cp.wait()

        @pl.when(g + 1 < pl.num_programs(0) * n_seg)
        def _():
            next_pass = jnp.where(seg_ref[b] + 1 == n_seg, p + 1, p)
            for cp in tile_copies(nx_ref[b], next_pass, 1 - slot):
                cp.start(priority=WEIGHT_DMA_PRIORITY)

        for k in range(len(w_hbms)):
            wbf_ref[k] = wbuf_ref[slot, k].astype(BF16)


def _expert_up_kernel(be_ref, fl_ref, nx_ref, seg_ref, nseg_ref, x_ref, wg_hbm, wu_hbm, o_ref, wbuf_ref, wbf_ref, sem):
    _stream_expert_weights((be_ref, fl_ref, nx_ref, seg_ref, nseg_ref), (wg_hbm, wu_hbm), wbuf_ref, wbf_ref, sem,
                           o_ref.shape[1])
    valid = jnp.bitwise_and(fl_ref[pl.program_id(1)], BLOCK_VALID) != 0

    @pl.when(valid)
    def _():
        x_hi, x_lo = _unpack_bf16_pairs(x_ref[...])
        half = x_hi.shape[1]
        dot = functools.partial(jnp.dot, preferred_element_type=F32)
        g = dot(x_hi, wbf_ref[0, 0:half, :]) + dot(x_lo, wbf_ref[0, half:2 * half, :])
        u = dot(x_hi, wbf_ref[1, 0:half, :]) + dot(x_lo, wbf_ref[1, half:2 * half, :])
        o_ref[...] = (_silu(g) * u).astype(o_ref.dtype)

    @pl.when(jnp.logical_not(valid))
    def _():
        o_ref[...] = jnp.zeros(o_ref.shape, o_ref.dtype)


def _expert_up(xs, plan, w_gate, w_up):
    n_rows = xs.shape[0]
    d, ff = w_gate.shape[1], w_gate.shape[2]
    tf = _pick(ff, (512, 256, 128))
    return pl.pallas_call(
        _expert_up_kernel,
        grid_spec=pltpu.PrefetchScalarGridSpec(
            num_scalar_prefetch=5,
            grid=(ff // tf, n_rows // EXPERT_BLOCK),
            in_specs=[pl.BlockSpec((EXPERT_BLOCK, d // 2), lambda f, b, *_: (b, 0)),
                      pl.BlockSpec(memory_space=pl.ANY), pl.BlockSpec(memory_space=pl.ANY)],
            out_specs=pl.BlockSpec((EXPERT_BLOCK, tf), lambda f, b, *_: (b, f)),
            scratch_shapes=[pltpu.VMEM((2, 2, d, tf), F32), pltpu.VMEM((2, d, tf), BF16),
                            pltpu.SemaphoreType.DMA((2, 2))]),
        out_shape=jax.ShapeDtypeStruct((n_rows, ff), BF16),
        compiler_params=_params("arbitrary", "arbitrary"),
        name="expert_up",
    )(*plan, xs, w_gate, w_up)


def _expert_down_kernel(be_ref, fl_ref, nx_ref, seg_ref, nseg_ref, h_ref, wd_hbm, o_ref, wbuf_ref, wbf_ref, sem):
    _stream_expert_weights((be_ref, fl_ref, nx_ref, seg_ref, nseg_ref), (wd_hbm,), wbuf_ref, wbf_ref, sem,
                           o_ref.shape[1])
    valid = jnp.bitwise_and(fl_ref[pl.program_id(1)], BLOCK_VALID) != 0

    @pl.when(valid)
    def _():
        o_ref[...] = jnp.dot(h_ref[...], wbf_ref[0], preferred_element_type=F32)

    @pl.when(jnp.logical_not(valid))
    def _():
        o_ref[...] = jnp.zeros(o_ref.shape, o_ref.dtype)


def _expert_down(hb, plan, w_down):
    n_rows, ff = hb.shape
    d = w_down.shape[2]
    tn = _pick(d, (2048, 1024, 512, 256, 128))
    return pl.pallas_call(
        _expert_down_kernel,
        grid_spec=pltpu.PrefetchScalarGridSpec(
            num_scalar_prefetch=5,
            grid=(d // tn, n_rows // EXPERT_BLOCK),
            in_specs=[pl.BlockSpec((EXPERT_BLOCK, ff), lambda n, b, *_: (b, 0)),
                      pl.BlockSpec(memory_space=pl.ANY)],
            out_specs=pl.BlockSpec((EXPERT_BLOCK, tn), lambda n, b, *_: (b, n)),
            scratch_shapes=[pltpu.VMEM((2, 1, ff, tn), F32), pltpu.VMEM((1, ff, tn), BF16),
                            pltpu.SemaphoreType.DMA((2, 1))]),
        out_shape=jax.ShapeDtypeStruct((n_rows, d), F32),
        compiler_params=_params("arbitrary", "arbitrary"),
        name="expert_down",
    )(*plan, hb, w_down)


def _combine_ln_kernel(d0_ref, d1_ref, ys_hbm, h_ref, gate_ref, g_ref, b_ref, op_ref, os_ref, buf_ref, sem,
                       *, alpha, prompt_blocks):
    tm = h_ref.shape[0]
    i = pl.program_id(0)
    base = i * tm

    def row_copy(k, r, src_row):
        return pltpu.make_async_copy(ys_hbm.at[pl.ds(src_row, 1), :], buf_ref.at[k, pl.ds(r, 1), :], sem)

    def start(r, carry):
        row_copy(0, r, d0_ref[base + r]).start(priority=0)
        row_copy(1, r, d1_ref[base + r]).start(priority=1)
        return carry

    def wait(r, carry):
        row_copy(0, r, 0).wait()
        row_copy(1, r, 0).wait()
        return carry

    lax.fori_loop(0, tm, start, 0)
    lax.fori_loop(0, tm, wait, 0)
    ffn = buf_ref[0] * gate_ref[:, 0:1] + buf_ref[1] * gate_ref[:, 1:2]
    y = _ln_rows(alpha * h_ref[...] + ffn, g_ref[...], b_ref[...])

    @pl.when(i < prompt_blocks)
    def _():
        op_ref[...] = y

    @pl.when(i >= prompt_blocks)
    def _():
        os_ref[...] = y


def _combine_ln(ys, dest0, dest1, h, gate, g, b, alpha, n_prompt):
    n, d = h.shape
    tm = _pick(math.gcd(n_prompt, n - n_prompt), (128, 64, 8))
    pb = n_prompt // tm
    row = pl.BlockSpec((tm, d), lambda i, d0, d1: (i, 0))
    vec = pl.BlockSpec((1, d), lambda i, d0, d1: (0, 0))
    return pl.pallas_call(
        functools.partial(_combine_ln_kernel, alpha=alpha, prompt_blocks=pb),
        grid_spec=pltpu.PrefetchScalarGridSpec(
            num_scalar_prefetch=2,
            grid=(n // tm,),
            in_specs=[pl.BlockSpec(memory_space=pl.ANY), row,
                      pl.BlockSpec((tm, LANES), lambda i, d0, d1: (i, 0)), vec, vec],
            out_specs=[pl.BlockSpec((tm, d), lambda i, d0, d1: (jnp.minimum(i, pb - 1), 0)),
                       pl.BlockSpec((tm, d), lambda i, d0, d1: (jnp.maximum(i - pb, 0), 0))],
            scratch_shapes=[pltpu.VMEM((2, tm, d), F32), pltpu.SemaphoreType.DMA(())]),
        out_shape=[jax.ShapeDtypeStruct((n_prompt, d), F32), jax.ShapeDtypeStruct((n - n_prompt, d), F32)],
        compiler_params=_params("arbitrary"),
        name="combine_ln",
    )(dest0, dest1, ys, h, gate, g.reshape(1, d), b.reshape(1, d))


def _dispatch_plan(eid, n_experts):
    n, k = eid.shape
    flat_e = eid.reshape(-1)
    onehot = (flat_e[:, None] == jnp.arange(n_experts, dtype=jnp.int32)[None, :]).astype(jnp.int32)
    csum = jnp.cumsum(onehot, axis=0)
    rank = jnp.sum(csum * onehot, axis=1) - 1
    counts = csum[-1]
    padded = (counts + EXPERT_BLOCK - 1) // EXPERT_BLOCK * EXPERT_BLOCK
    p_end = jnp.cumsum(padded)
    p_start = p_end - padded
    dest = (p_start[flat_e] + rank).astype(jnp.int32)
    n_blocks = -(-(n * k + n_experts * (EXPERT_BLOCK - 1)) // EXPERT_BLOCK)
    flat_tok = jnp.repeat(jnp.arange(n, dtype=jnp.int32), k)
    row_tok = jnp.zeros((n_blocks * EXPERT_BLOCK,), jnp.int32).at[dest].set(flat_tok, unique_indices=True)
    blk_row = jnp.arange(n_blocks, dtype=jnp.int32) * EXPERT_BLOCK
    blk_e = jnp.minimum(jnp.searchsorted(p_end, blk_row, side='right'), n_experts - 1).astype(jnp.int32)
    valid = blk_row < p_end[-1]
    blk_e = jnp.where(valid, blk_e, blk_e[p_end[-1] // EXPERT_BLOCK - 1])
    blk_cnt = jnp.clip(counts[blk_e] - (blk_row - p_start[blk_e]), 0, EXPERT_BLOCK).astype(jnp.int32)
    new_e = jnp.concatenate([jnp.ones((1,), bool), blk_e[1:] != blk_e[:-1]])
    flags = (valid.astype(jnp.int32) * BLOCK_VALID + new_e.astype(jnp.int32) * BLOCK_NEW_EXPERT)
    seg = jnp.cumsum(new_e.astype(jnp.int32)) - 1
    n_seg = seg[-1:] + 1
    seg_e = jnp.zeros((n_experts,), jnp.int32).at[seg].set(blk_e)
    next_e = seg_e[(seg + 1) % n_seg]
    plan = (blk_e, flags, next_e.astype(jnp.int32), seg.astype(jnp.int32), n_seg.astype(jnp.int32))
    return row_tok, blk_cnt, plan, dest.reshape(n, k)


def _pad_rows(a, rows):
    return jnp.pad(a, ((0, 0), (0, rows - a.shape[1]), (0, 0)))


def _layer(x_p, x_s, mem_prompt, win_k, win_v, conv_state, mem_k_s, mem_v_s,
           w_in, conv_w, conv_b, conv_ln_g, conv_ln_b, w_out, ln1_g, ln1_b,
           w_mem_q, w_mem_k, w_mem_v, w_mem_o, ln2_g, ln2_b,
           w_rg, b_rg, w_re, b_re, w_gate, w_up, w_down, ln3_g, ln3_b, alpha):
    bp, t, d = x_p.shape
    bs, ts, _ = x_s.shape
    past, heads = win_k.shape[1], win_k.shape[2]
    width = heads * HEAD_DIM
    conv_ch = conv_state.shape[2]
    cs = conv_ch // LANES
    mem_tokens, mem_heads = mem_k_s.shape[1], mem_k_s.shape[2]
    n_experts = w_gate.shape[0]
    assert bp == 1 and t % W_MAX == 0 and past == W_MAX and win_k.shape[3] == HEAD_DIM
    assert ts <= SAMPLE_U_PAD and w_in.shape[1] == 3 * width + 2 * conv_ch and width + conv_ch == d
    n_s = bs * ts

    x = jnp.concatenate([x_p.reshape(t, d), x_s.reshape(n_s, d)], axis=0)
    proj3 = _matmul_slabs(x, w_in.astype(BF16))

    def sample_part(lo, hi):
        return jnp.transpose(proj3[lo:hi, t:], (1, 0, 2)).reshape(bs, ts, (hi - lo) * LANES)

    q_s = sample_part(0, heads)
    k_s = sample_part(heads, 2 * heads)
    v_s = sample_part(2 * heads, 3 * heads)
    ga_s = sample_part(3 * heads, 3 * heads + cs)
    gb_s = sample_part(3 * heads + cs, 3 * heads + 2 * cs)

    put_sample = lambda full, rows: lax.dynamic_update_slice(full, rows.astype(full.dtype), (t,) + (0,) * (full.ndim - 1))
    attn_p, attn_s, win_k_s, win_v_s = _attn(proj3, t, heads, _pad_rows(q_s, SAMPLE_Q_PAD),
                                             k_s.reshape(bs, ts, heads, HEAD_DIM),
                                             v_s.reshape(bs, ts, heads, HEAD_DIM), win_k, win_v)
    attn = put_sample(attn_p, attn_s[:, :ts].reshape(n_s, width))
    keep_p = min(W_MAX, t)
    win_k_p = jnp.transpose(proj3[heads:2 * heads, t - keep_p:t], (1, 0, 2))[None]
    win_v_p = jnp.transpose(proj3[2 * heads:3 * heads, t - keep_p:t], (1, 0, 2))[None]

    conv_p, u_last = _conv_prompt(proj3, t, 3 * heads, 3 * heads + cs, cs, conv_w, conv_b)
    conv_state_p = u_last[CONV_HALO - (CONV_W - 1):][None]
    conv_s, conv_state_s = _conv_sample(conv_state, _pad_rows(ga_s, SAMPLE_U_PAD), _pad_rows(gb_s, SAMPLE_U_PAD),
                                        conv_w, conv_b, ts)
    conv_all = put_sample(conv_p, conv_s[:, :ts].reshape(n_s, conv_ch))
    c_all = _ln_silu(conv_all, conv_ln_g, conv_ln_b)

    h1, h1_bf = _res_ln(x, _matmul([attn, c_all], w_out.astype(BF16)), ln1_g, ln1_b, alpha)

    qm = _matmul([h1_bf], w_mem_q.astype(BF16), out_dtype=BF16)
    mem_x = mem_prompt.reshape(bp * mem_tokens, d)
    mem_k_p = _matmul([mem_x], w_mem_k.astype(BF16))
    mem_v_p = _matmul([mem_x], w_mem_v.astype(BF16))
    om_s = _mem_attn(_pad_rows(qm[t:].reshape(bs, ts, d), SAMPLE_Q_PAD), mem_k_s.reshape(bs, mem_tokens, d),
                     mem_v_s.reshape(bs, mem_tokens, d), mem_heads)[:, :ts]
    om = put_sample(_mem_attn(qm[None], mem_k_p[None], mem_v_p[None], mem_heads)[0], om_s.reshape(n_s, d))
    h2, h2_packed = _res_ln(h1, _matmul([om], w_mem_o.astype(BF16)), ln2_g, ln2_b, alpha, pack=True)

    eid, gate = _router(h2, w_rg, b_rg, w_re, b_re)
    row_tok, blk_cnt, plan, dest = _dispatch_plan(eid, n_experts)
    xs = _gather_rows(h2_packed, row_tok, blk_cnt)
    hb = _expert_up(xs, plan, w_gate, w_up)
    ys = _expert_down(hb, plan, w_down)
    y_p, y_s = _combine_ln(ys, dest[:, 0], dest[:, 1], h2, gate, ln3_g, ln3_b, alpha, t)

    mem_shape = (bp, mem_tokens, mem_heads, d // mem_heads)
    return (y_p.reshape(bp, t, d), y_s.reshape(bs, ts, d), win_k_p, win_v_p, conv_state_p,
            mem_k_p.reshape(mem_shape), mem_v_p.reshape(mem_shape), win_k_s, win_v_s, conv_state_s)


def kernel(x_prompt, x_sample, mem_prompt, cache_win_k, cache_win_v, state_conv, cache_mem_k, cache_mem_v, w_in, conv_w, conv_b, conv_ln_g, conv_ln_b, w_out, ln1_g, ln1_b, w_mem_q, w_mem_k, w_mem_v, w_mem_o, ln2_g, ln2_b, w_router_group, b_router_group, w_router_expert, b_router_expert, w_exp_gate, w_exp_up, w_exp_down, ln3_g, ln3_b):
    depth = w_in.shape[0]
    alpha = (2 * depth) ** 0.25
    hp, hs = x_prompt, x_sample
    per_layer = []
    for l in range(depth):
        outs = _layer(hp, hs, mem_prompt, cache_win_k[l], cache_win_v[l], state_conv[l], cache_mem_k[l], cache_mem_v[l],
                      w_in[l], conv_w[l], conv_b[l], conv_ln_g[l], conv_ln_b[l], w_out[l], ln1_g[l], ln1_b[l],
                      w_mem_q[l], w_mem_k[l], w_mem_v[l], w_mem_o[l], ln2_g[l], ln2_b[l],
                      w_router_group[l], b_router_group[l], w_router_expert[l], b_router_expert[l],
                      w_exp_gate[l], w_exp_up[l], w_exp_down[l], ln3_g[l], ln3_b[l], alpha)
        hp, hs = outs[0], outs[1]
        per_layer.append(outs[2:])
    stacked = [jnp.stack([layer[i] for layer in per_layer]) for i in range(8)]
    return (hp, hs, *stacked)
```

```python
import functools
import math

import jax
import jax.numpy as jnp
from jax import lax
from jax.experimental import pallas as pl
from jax.experimental.pallas import tpu as pltpu

F32 = jnp.float32
BF16 = jnp.bfloat16

LANES = 128
HEAD_DIM = 128
DILATED_BRANCHES = ((128, 1), (512, 4), (2048, 16))
W_MAX = 2048
KEYS_PER_BLOCK = 128
ATTN_GROUP = 8
CONV_W = 31
CONV_HALO = 32
EXPERT_BLOCK = 128
LN_EPS = 1e-5
VMEM_LIMIT = 56 * 1024 * 1024


def _params(*sem):
    return pltpu.CompilerParams(dimension_semantics=sem, vmem_limit_bytes=VMEM_LIMIT)


def _pick(n, candidates):
    for c in candidates:
        if n % c == 0:
            return c
    return n


def _mm_kernel(*refs):
    a_refs, b_ref, o_ref = refs[:-2], refs[-2], refs[-1]
    acc = None
    k0 = 0
    for a_ref in a_refs:
        k1 = k0 + a_ref.shape[1]
        part = jnp.dot(a_ref[...].astype(BF16), b_ref[k0:k1, :].astype(BF16), preferred_element_type=F32)
        acc = part if acc is None else acc + part
        k0 = k1
    o_ref[...] = acc.astype(o_ref.dtype)


def _matmul(srcs, b, out_dtype=F32):
    m = srcs[0].shape[0]
    k, n = b.shape
    assert sum(a.shape[1] for a in srcs) == k
    tm = _pick(m, (640, 512, 256, 128))
    tn = _pick(n, (512, 256, 128))
    return pl.pallas_call(
        _mm_kernel,
        grid=(m // tm, n // tn),
        in_specs=[pl.BlockSpec((tm, a.shape[1]), lambda i, j: (i, 0)) for a in srcs]
        + [pl.BlockSpec((k, tn), lambda i, j: (0, j))],
        out_specs=pl.BlockSpec((tm, tn), lambda i, j: (i, j)),
        out_shape=jax.ShapeDtypeStruct((m, n), out_dtype),
        compiler_params=_params("parallel", "arbitrary"),
        name="matmul",
    )(*srcs, b)


def _mm_slab_kernel(a_ref, b_ref, o_ref):
    r = jnp.dot(a_ref[...].astype(BF16), b_ref[...].astype(BF16), preferred_element_type=F32)
    for s in range(o_ref.shape[0]):
        o_ref[s] = r[:, s * LANES:(s + 1) * LANES]


def _matmul_slabs(a, b):
    m, k = a.shape
    n = b.shape[1]
    tm = _pick(m, (640, 512, 256, 128))
    tn = _pick(n, (512, 256, 128))
    return pl.pallas_call(
        _mm_slab_kernel,
        grid=(m // tm, n // tn),
        in_specs=[pl.BlockSpec((tm, k), lambda i, j: (i, 0)),
                  pl.BlockSpec((k, tn), lambda i, j: (0, j))],
        out_specs=pl.BlockSpec((tn // LANES, tm, LANES), lambda i, j: (j, i, 0)),
        out_shape=jax.ShapeDtypeStruct((n // LANES, m, LANES), F32),
        compiler_params=_params("parallel", "arbitrary"),
        name="matmul_slabs",
    )(a, b)


def _ln_rows(x, g, b):
    mu = jnp.mean(x, axis=-1, keepdims=True)
    xc = x - mu
    var = jnp.mean(xc * xc, axis=-1, keepdims=True)
    return xc * lax.rsqrt(var + LN_EPS) * g + b


def _silu(x):
    return x * jax.nn.sigmoid(x)


def _pack_bf16_pairs(hi, lo):
    hi = lax.bitcast_convert_type(hi.astype(BF16).astype(F32), jnp.uint32)
    lo = lax.bitcast_convert_type(lo.astype(BF16).astype(F32), jnp.uint32)
    return jnp.bitwise_or(hi, jnp.right_shift(lo, jnp.uint32(16)))


def _unpack_bf16_pairs(u):
    hi = lax.bitcast_convert_type(jnp.bitwise_and(u, jnp.uint32(0xFFFF0000)), F32)
    lo = lax.bitcast_convert_type(jnp.left_shift(u, jnp.uint32(16)), F32)
    return hi, lo


def _res_ln_kernel(res_ref, t_ref, g_ref, b_ref, o_ref, o2_ref, *, alpha, pack):
    y = _ln_rows(alpha * res_ref[...] + t_ref[...], g_ref[...], b_ref[...])
    o_ref[...] = y
    half = y.shape[1] // 2
    o2_ref[...] = _pack_bf16_pairs(y[:, :half], y[:, half:]) if pack else y.astype(BF16)


def _res_ln(res, t, g, b, alpha, pack=False):
    n, d = res.shape
    tm = _pick(n, (128, 64, 8))
    row = pl.BlockSpec((tm, d), lambda i: (i, 0))
    vec = pl.BlockSpec((1, d), lambda i: (0, 0))
    second = (jax.ShapeDtypeStruct((n, d // 2), jnp.uint32), pl.BlockSpec((tm, d // 2), lambda i: (i, 0))) if pack \
        else (jax.ShapeDtypeStruct((n, d), BF16), row)
    return pl.pallas_call(
        functools.partial(_res_ln_kernel, alpha=alpha, pack=pack),
        grid=(n // tm,),
        in_specs=[row, row, vec, vec],
        out_specs=[row, second[1]],
        out_shape=[jax.ShapeDtypeStruct((n, d), F32), second[0]],
        compiler_params=_params("parallel"),
        name="res_ln",
    )(res, t, g.reshape(1, d), b.reshape(1, d))


def _ln_silu_kernel(x_ref, g_ref, b_ref, o_ref):
    o_ref[...] = _silu(_ln_rows(x_ref[...], g_ref[...], b_ref[...])).astype(o_ref.dtype)


def _ln_silu(x, g, b):
    n, d = x.shape
    tm = _pick(n, (256, 128, 64, 8))
    row = pl.BlockSpec((tm, d), lambda i: (i, 0))
    vec = pl.BlockSpec((1, d), lambda i: (0, 0))
    return pl.pallas_call(
        _ln_silu_kernel,
        grid=(n // tm,),
        in_specs=[row, vec, vec],
        out_specs=row,
        out_shape=jax.ShapeDtypeStruct((n, d), BF16),
        compiler_params=_params("parallel"),
        name="ln_silu",
    )(x, g.reshape(1, d), b.reshape(1, d))


def _attn_prompt_body(c, n_chunks, q_ref, kp_ref, kc_ref, vp_ref, vc_ref, o_ref, acc_ref, m_ref, l_ref):
    scale = 1.0 / math.sqrt(HEAD_DIM)
    blk = KEYS_PER_BLOCK
    ii = lax.broadcasted_iota(jnp.int32, (blk, blk), 0)
    jj = lax.broadcasted_iota(jnp.int32, (blk, blk), 1)
    nt = (((1,), (1,)), ((), ()))

    def rows(start, d):
        return pl.ds(start, blk) if d == 1 else pl.ds(start, blk, stride=d)

    def sub_blocks(subs, d, first, last):
        n = range(len(subs))
        cur = [rows(r + d * blk * b, d) for r, b in subs]
        prev = [rows(r + W_MAX - d * blk, d) if b == 0 else rows(r + d * blk * (b - 1), d) for r, b in subs]
        k_prev = [kp_ref if b == 0 else kc_ref for _, b in subs]
        v_prev = [vp_ref if b == 0 else vc_ref for _, b in subs]
        q = [(q_ref[cur[i], :] * scale).astype(BF16) for i in n]
        s_a = [lax.dot_general(q[i], k_prev[i][prev[i], :].astype(BF16), nt, preferred_element_type=F32) for i in n]
        s_b = [lax.dot_general(q[i], kc_ref[cur[i], :].astype(BF16), nt, preferred_element_type=F32) for i in n]
        for i, (_, b) in enumerate(subs):
            mask_a = jj >= ii
            if b == 0:
                mask_a = jnp.logical_and(mask_a, c > 0)
            s_a[i] = jnp.where(mask_a, s_a[i], -jnp.inf)
            s_b[i] = jnp.where(jj <= ii, s_b[i], -jnp.inf)
        m_loc = [jnp.maximum(jnp.max(s_a[i], axis=1, keepdims=True), jnp.max(s_b[i], axis=1, keepdims=True))
                 for i in n]
        p_a = [jnp.exp(s_a[i] - m_loc[i]) for i in n]
        p_b = [jnp.exp(s_b[i] - m_loc[i]) for i in n]
        l_loc = [jnp.sum(p_a[i], axis=1, keepdims=True) + jnp.sum(p_b[i], axis=1, keepdims=True) for i in n]
        acc_loc = [jnp.dot(p_a[i].astype(BF16), v_prev[i][prev[i], :].astype(BF16), preferred_element_type=F32)
                   + jnp.dot(p_b[i].astype(BF16), vc_ref[cur[i], :].astype(BF16), preferred_element_type=F32)
                   for i in n]
        for i in n:
            if first:
                m_new, l_new, acc_new = m_loc[i], l_loc[i], acc_loc[i]
            else:
                m_old = m_ref[cur[i], :][:, :1]
                l_old = l_ref[cur[i], :][:, :1]
                m_new = jnp.maximum(m_old, m_loc[i])
                a_old = jnp.exp(m_old - m_new)
                a_loc = jnp.exp(m_loc[i] - m_new)
                l_new = a_old * l_old + a_loc * l_loc[i]
                acc_new = a_old * acc_ref[cur[i], :] + a_loc * acc_loc[i]
            if last:
                o_ref[cur[i], :] = (acc_new / l_new).astype(o_ref.dtype)
            else:
                m_ref[cur[i], :] = jnp.broadcast_to(m_new, (blk, LANES))
                l_ref[cur[i], :] = jnp.broadcast_to(l_new, (blk, LANES))
                acc_ref[cur[i], :] = acc_new

    @pl.when(c < n_chunks)
    def _():
        order = sorted(DILATED_BRANCHES, key=lambda wd: -wd[1])
        for idx, (w, d) in enumerate(order):
            subs = [(r, b) for b in range(W_MAX // (d * blk)) for r in range(d)]
            for g in range(0, len(subs), ATTN_GROUP):
                sub_blocks(subs[g:g + ATTN_GROUP], d, idx == 0, idx == len(order) - 1)

    @pl.when(c >= n_chunks)
    def _():
        o_ref[...] = jnp.zeros(o_ref.shape, o_ref.dtype)


SAMPLE_Q_PAD = 8
SAMPLE_HEADS = 8


def _key_multiplicity(dist):
    mult = jnp.zeros(dist.shape, F32)
    for (w, d) in DILATED_BRANCHES:
        hit = jnp.logical_and(jnp.bitwise_and(dist, d - 1) == 0, dist <= w)
        mult = mult + jnp.where(hit, 1.0, 0.0)
    return jnp.where(dist >= 0, mult, 0.0)


def _window_copies(b, g, kn_ref, vn_ref, kc_ref, vc_ref, wk_hbm, wv_hbm, sem):
    _, past, hg, _ = kc_ref.shape
    t_new = kn_ref.shape[1]
    heads = pl.ds(pl.multiple_of(g * hg, hg), hg)
    copies = []
    for k, (c_ref, n_ref, w_hbm) in enumerate(((kc_ref, kn_ref, wk_hbm), (vc_ref, vn_ref, wv_hbm))):
        copies.append(pltpu.make_async_copy(c_ref.at[0, pl.ds(t_new, past - t_new)],
                                            w_hbm.at[b, pl.ds(0, past - t_new), heads, :], sem.at[2 * k]))
        copies.append(pltpu.make_async_copy(n_ref.at[0], w_hbm.at[b, pl.ds(past - t_new, t_new), heads, :],
                                            sem.at[2 * k + 1]))
    return copies


def _attn_sample_body(q_ref, kn_ref, vn_ref, kc_ref, vc_ref, o_ref):
    _, past, hg, _ = kc_ref.shape
    t_new = kn_ref.shape[1]
    scale = 1.0 / math.sqrt(HEAD_DIM)
    qi = lax.broadcasted_iota(jnp.int32, (SAMPLE_Q_PAD, past), 0)
    rho = lax.broadcasted_iota(jnp.int32, (SAMPLE_Q_PAD, past), 1)
    mult_c = _key_multiplicity(past + qi - rho)
    qi1 = lax.broadcasted_iota(jnp.int32, (SAMPLE_Q_PAD, 1), 0)
    mult_n = [_key_multiplicity(qi1 - i) for i in range(t_new)]
    nt = (((1,), (1,)), ((), ()))
    k_rows = kc_ref.at[0].reshape(past * hg, HEAD_DIM)
    v_rows = vc_ref.at[0].reshape(past * hg, HEAD_DIM)
    for h in range(hg):
        sl = slice(h * HEAD_DIM, (h + 1) * HEAD_DIM)
        head_rows = pl.ds(h, past, stride=hg)
        q = q_ref[0, :, sl] * scale
        s_c = lax.dot_general(q.astype(BF16), k_rows[head_rows, :].astype(BF16), nt, preferred_element_type=F32)
        s_c = jnp.where(mult_c > 0, s_c, -jnp.inf)
        s_n = [jnp.sum(q * kn_ref[0, i, h:h + 1, :], axis=1, keepdims=True) for i in range(t_new)]
        s_n = [jnp.where(mult_n[i] > 0, s_n[i], -jnp.inf) for i in range(t_new)]
        m = jnp.max(s_c, axis=1, keepdims=True)
        for i in range(t_new):
            m = jnp.maximum(m, s_n[i])
        p_c = mult_c * jnp.exp(s_c - m)
        l = jnp.sum(p_c, axis=1, keepdims=True)
        o = jnp.dot(p_c.astype(BF16), v_rows[head_rows, :].astype(BF16), preferred_element_type=F32)
        for i in range(t_new):
            p_n = mult_n[i] * jnp.exp(s_n[i] - m)
            l = l + p_n
            o = o + p_n * vn_ref[0, i, h:h + 1, :]
        o_ref[0, :, sl] = o / l


def _attn_kernel(*refs, n_chunks, heads, prompt_steps, sample_steps, sample_groups):
    prompt_in, sample_in = refs[0:5], refs[5:10]
    o_ref, so_ref, wk_hbm, wv_hbm, acc_ref, m_ref, l_ref, sem = refs[10:]
    s = pl.program_id(0)
    window_copies = functools.partial(_window_copies, s // sample_groups, s % sample_groups, *sample_in[1:],
                                      wk_hbm, wv_hbm, sem)

    @pl.when(s < sample_steps)
    def _():
        for cp in window_copies():
            cp.start()
        _attn_sample_body(*sample_in, so_ref)

    @pl.when(s < prompt_steps)
    def _():
        _attn_prompt_body(s // heads, n_chunks, *prompt_in, o_ref, acc_ref, m_ref, l_ref)

    @pl.when(s < sample_steps)
    def _():
        for cp in window_copies():
            cp.wait()


def _attn(proj3, t, heads, q, k_new, v_new, k_cache, v_cache):
    n_total = proj3.shape[1]
    n_chunks = t // W_MAX
    bsz, past, _, _ = k_cache.shape
    t_new = k_new.shape[1]
    hg = SAMPLE_HEADS if heads % SAMPLE_HEADS == 0 else heads
    groups = heads // hg
    prompt_steps = pl.cdiv(n_total, W_MAX) * heads
    sample_steps = bsz * groups

    ps = lambda s: jnp.minimum(s, prompt_steps - 1)
    chunk = lambda s: jnp.minimum(ps(s) // heads, n_chunks - 1)
    blk = (None, W_MAX, LANES)
    cur = lambda off: pl.BlockSpec(blk, lambda s: (off + ps(s) % heads, chunk(s), 0))
    prev = lambda off: pl.BlockSpec(blk, lambda s: (off + ps(s) % heads, jnp.maximum(chunk(s) - 1, 0), 0))
    ss = lambda s: jnp.minimum(s, sample_steps - 1)
    qspec = pl.BlockSpec((1, SAMPLE_Q_PAD, hg * HEAD_DIM), lambda s: (ss(s) // groups, 0, ss(s) % groups))
    kv = lambda rows: pl.BlockSpec((1, rows, hg, HEAD_DIM), lambda s: (ss(s) // groups, 0, ss(s) % groups, 0))
    win = jax.ShapeDtypeStruct(k_cache.shape, F32)
    return pl.pallas_call(
        functools.partial(_attn_kernel, n_chunks=n_chunks, heads=heads, prompt_steps=prompt_steps,
                          sample_steps=sample_steps, sample_groups=groups),
        grid=(max(prompt_steps, sample_steps),),
        in_specs=[cur(0), prev(heads), cur(heads), prev(2 * heads), cur(2 * heads),
                  qspec, kv(t_new), kv(t_new), kv(past), kv(past)],
        out_specs=[pl.BlockSpec((W_MAX, LANES), lambda s: (ps(s) // heads, ps(s) % heads)), qspec,
                   pl.BlockSpec(memory_space=pl.ANY), pl.BlockSpec(memory_space=pl.ANY)],
        out_shape=[jax.ShapeDtypeStruct((n_total, heads * LANES), BF16),
                   jax.ShapeDtypeStruct((bsz, SAMPLE_Q_PAD, heads * HEAD_DIM), F32), win, win],
        scratch_shapes=[pltpu.VMEM((W_MAX, LANES), F32)] * 3 + [pltpu.SemaphoreType.DMA((4,))],
        compiler_params=_params("arbitrary"),
        name="attn",
    )(proj3, proj3, proj3, proj3, proj3, q, k_new, v_new, k_cache, v_cache)


CONV_ROWS = 64


def _conv_taps(ubuf_ref, w_ref, cb_ref, out_ref, n_rows, base):
    for r0 in range(0, n_rows, CONV_ROWS):
        nr = min(CONV_ROWS, n_rows - r0)
        acc = jnp.broadcast_to(cb_ref[...], (nr, LANES))
        for j in range(CONV_W):
            acc = acc + ubuf_ref[pl.ds(base + r0 + j, nr), :] * w_ref[j:j + 1, :]
        out_ref[pl.ds(r0, nr), :] = acc


def _conv_prompt_kernel(ga_ref, gb_ref, w_ref, cb_ref, o_ref, ulast_ref, ubuf_ref, *, n_blocks):
    i = pl.program_id(1)
    tb = ga_ref.shape[0]

    @pl.when(i == 0)
    def _():
        ubuf_ref[0:CONV_HALO, :] = jnp.zeros((CONV_HALO, LANES), F32)

    @pl.when(jnp.logical_and(i > 0, i < n_blocks))
    def _():
        ubuf_ref[0:CONV_HALO, :] = ubuf_ref[tb:tb + CONV_HALO, :]

    @pl.when(i < n_blocks)
    def _():
        ubuf_ref[CONV_HALO:CONV_HALO + tb, :] = ga_ref[...] * jax.nn.sigmoid(gb_ref[...])
        _conv_taps(ubuf_ref, w_ref, cb_ref, o_ref, tb, CONV_HALO - (CONV_W - 1))

    @pl.when(i == n_blocks - 1)
    def _():
        ulast_ref[...] = ubuf_ref[tb:tb + CONV_HALO, :]

    @pl.when(i >= n_blocks)
    def _():
        o_ref[...] = jnp.zeros(o_ref.shape, o_ref.dtype)


def _conv_prompt(proj3, t, slab_a, slab_b, n_slabs, conv_w, conv_b):
    n_total = proj3.shape[1]
    tb = _pick(t, (512, 256, 128))
    c = n_slabs * LANES
    n_blocks = t // tb
    blk = lambda i: jnp.minimum(i, n_blocks - 1)
    return pl.pallas_call(
        functools.partial(_conv_prompt_kernel, n_blocks=n_blocks),
        grid=(n_slabs, pl.cdiv(n_total, tb)),
        in_specs=[pl.BlockSpec((None, tb, LANES), lambda s, i: (slab_a + s, blk(i), 0)),
                  pl.BlockSpec((None, tb, LANES), lambda s, i: (slab_b + s, blk(i), 0)),
                  pl.BlockSpec((CONV_W, LANES), lambda s, i: (0, s)),
                  pl.BlockSpec((1, LANES), lambda s, i: (0, s))],
        out_specs=[pl.BlockSpec((tb, LANES), lambda s, i: (i, s)),
                   pl.BlockSpec((CONV_HALO, LANES), lambda s, i: (0, s))],
        out_shape=[jax.ShapeDtypeStruct((n_total, c), F32), jax.ShapeDtypeStruct((CONV_HALO, c), F32)],
        scratch_shapes=[pltpu.VMEM((CONV_HALO + tb, LANES), F32)],
        compiler_params=_params("parallel", "arbitrary"),
        name="conv_prompt",
    )(proj3, proj3, conv_w, conv_b.reshape(1, c))


SAMPLE_U_PAD = 8


def _conv_sample_kernel(st_ref, ga_ref, gb_ref, w_ref, cb_ref, o_ref, nst_ref, ubuf_ref, *, t_new):
    bsz, hist, _ = st_ref.shape

    def one_sequence(b, carry):
        ubuf_ref[0:hist, :] = st_ref[b]
        ubuf_ref[hist:hist + SAMPLE_U_PAD, :] = ga_ref[b] * jax.nn.sigmoid(gb_ref[b])
        _conv_taps(ubuf_ref, w_ref, cb_ref, o_ref.at[b], SAMPLE_U_PAD, 0)
        nst_ref[b] = ubuf_ref[pl.ds(t_new, hist), :]
        return carry

    lax.fori_loop(0, bsz, one_sequence, 0)


def _conv_sample(state, ga, gb, conv_w, conv_b, t_new):
    bsz, hist, c = state.shape
    assert hist == CONV_W - 1 and t_new <= SAMPLE_U_PAD
    spec = lambda rows: pl.BlockSpec((bsz, rows, LANES), lambda s: (0, 0, s))
    return pl.pallas_call(
        functools.partial(_conv_sample_kernel, t_new=t_new),
        grid=(c // LANES,),
        in_specs=[spec(hist), spec(SAMPLE_U_PAD), spec(SAMPLE_U_PAD),
                  pl.BlockSpec((CONV_W, LANES), lambda s: (0, s)),
                  pl.BlockSpec((1, LANES), lambda s: (0, s))],
        out_specs=[spec(SAMPLE_U_PAD), spec(hist)],
        out_shape=[jax.ShapeDtypeStruct((bsz, SAMPLE_U_PAD, c), F32),
                   jax.ShapeDtypeStruct((bsz, hist, c), F32)],
        scratch_shapes=[pltpu.VMEM((hist + SAMPLE_U_PAD + 2, LANES), F32)],
        compiler_params=_params("parallel"),
        name="conv_sample",
    )(state, ga, gb, conv_w, conv_b.reshape(1, c))


def _mem_attn_kernel(q_ref, k_ref, v_ref, o_ref):
    hd = q_ref.shape[-1]
    nt = (((1,), (1,)), ((), ()))
    s = lax.dot_general(q_ref[...].astype(BF16), k_ref[...].astype(BF16), nt,
                        preferred_element_type=F32) * (1.0 / math.sqrt(hd))
    m = jnp.max(s, axis=1, keepdims=True)
    p = jnp.exp(s - m)
    l = jnp.sum(p, axis=1, keepdims=True)
    o = jnp.dot(p.astype(BF16), v_ref[...].astype(BF16), preferred_element_type=F32)
    o_ref[...] = (o / l).astype(o_ref.dtype)


def _mem_attn(q, k, v, heads):
    bsz, tq_all, d = q.shape
    m = k.shape[1]
    hd = d // heads
    tq = 512 if tq_all >= 512 else tq_all
    qspec = pl.BlockSpec((None, tq, hd), lambda b, h, i: (b, i, h))
    kspec = pl.BlockSpec((None, m, hd), lambda b, h, i: (b, 0, h))
    return pl.pallas_call(
        _mem_attn_kernel,
        grid=(bsz, heads, pl.cdiv(tq_all, tq)),
        in_specs=[qspec, kspec, kspec],
        out_specs=qspec,
        out_shape=jax.ShapeDtypeStruct((bsz, tq_all, d), BF16),
        compiler_params=_params("parallel", "parallel", "arbitrary"),
        name="mem_attn",
    )(q, k, v)


def _router_kernel(x_ref, w_ref, b_ref, eid_ref, gate_ref, *, n_groups, epg):
    logits = jnp.dot(x_ref[...].astype(BF16), w_ref[...].astype(BF16), preferred_element_type=F32) + b_ref[...]
    lane = lax.broadcasted_iota(jnp.int32, logits.shape, 1).astype(F32)
    neg = -jnp.inf
    none = float(LANES)

    def first_max(vals):
        top = jnp.max(vals, axis=1, keepdims=True)
        idx = jnp.min(jnp.where(vals == top, lane, none), axis=1, keepdims=True)
        return top, idx

    g_logits = jnp.where(lane < n_groups, logits, neg)
    g_top, grp = first_max(g_logits)
    p_grp = 1.0 / jnp.sum(jnp.exp(g_logits - g_top), axis=1, keepdims=True)
    lo = n_groups + grp * epg
    e_logits = jnp.where(jnp.logical_and(lane >= lo, lane < lo + epg), logits, neg)
    v1, i1 = first_max(e_logits)
    v2, i2 = first_max(jnp.where(lane == i1, neg, e_logits))
    e21 = jnp.exp(v2 - v1)
    g1 = p_grp * (1.0 / (1.0 + e21))
    g2 = p_grp * (e21 / (1.0 + e21))
    eid = jnp.where(lane == 0.0, i1 - n_groups, jnp.where(lane == 1.0, i2 - n_groups, 0.0))
    eid_ref[...] = eid.astype(jnp.int32)
    gate_ref[...] = jnp.where(lane == 0.0, g1, jnp.where(lane == 1.0, g2, 0.0))


def _router(x, w_rg, b_rg, w_re, b_re):
    n, d = x.shape
    n_groups, epg = w_re.shape[1], w_re.shape[2]
    n_log = n_groups + n_groups * epg
    assert n_log <= LANES
    w = jnp.concatenate([w_rg, w_re.reshape(d, n_groups * epg), jnp.zeros((d, LANES - n_log), F32)], axis=1)
    b = jnp.concatenate([b_rg, b_re.reshape(-1), jnp.zeros((LANES - n_log,), F32)]).reshape(1, LANES)
    tm = _pick(n, (640, 512, 256, 128))
    row = pl.BlockSpec((tm, LANES), lambda i: (i, 0))
    eid, gate = pl.pallas_call(
        functools.partial(_router_kernel, n_groups=n_groups, epg=epg),
        grid=(n // tm,),
        in_specs=[pl.BlockSpec((tm, d), lambda i: (i, 0)),
                  pl.BlockSpec((d, LANES), lambda i: (0, 0)),
                  pl.BlockSpec((1, LANES), lambda i: (0, 0))],
        out_specs=[row, row],
        out_shape=[jax.ShapeDtypeStruct((n, LANES), jnp.int32), jax.ShapeDtypeStruct((n, LANES), F32)],
        compiler_params=_params("parallel"),
        name="router",
    )(x, w, b)
    return eid[:, :2], gate


def _gather_rows_kernel(tok_ref, cnt_ref, x_hbm, o_ref, sem):
    b = pl.program_id(0)
    base = b * EXPERT_BLOCK
    cnt = cnt_ref[b]

    def row_copy(r, src_row):
        return pltpu.make_async_copy(x_hbm.at[pl.ds(src_row, 1), :], o_ref.at[pl.ds(r, 1), :], sem)

    def start_pair(i, carry):
        row_copy(2 * i, tok_ref[base + 2 * i]).start(priority=0)
        row_copy(2 * i + 1, tok_ref[base + 2 * i + 1]).start(priority=1)
        return carry

    def wait(r, carry):
        row_copy(r, 0).wait()
        return carry

    def zero_row(r, carry):
        o_ref[pl.ds(r, 1), :] = jnp.zeros((1, o_ref.shape[1]), o_ref.dtype)
        return carry

    lax.fori_loop(0, jnp.right_shift(cnt, 1), start_pair, 0)

    @pl.when(jnp.bitwise_and(cnt, 1) == 1)
    def _():
        row_copy(cnt - 1, tok_ref[base + cnt - 1]).start()

    lax.fori_loop(cnt, EXPERT_BLOCK, zero_row, 0)
    lax.fori_loop(0, cnt, wait, 0)


def _gather_rows(x, row_tok, blk_cnt):
    n_rows = row_tok.shape[0]
    d = x.shape[1]
    return pl.pallas_call(
        _gather_rows_kernel,
        grid_spec=pltpu.PrefetchScalarGridSpec(
            num_scalar_prefetch=2,
            grid=(n_rows // EXPERT_BLOCK,),
            in_specs=[pl.BlockSpec(memory_space=pl.ANY)],
            out_specs=pl.BlockSpec((EXPERT_BLOCK, d), lambda b, tok, cnt: (b, 0)),
            scratch_shapes=[pltpu.SemaphoreType.DMA(())]),
        out_shape=jax.ShapeDtypeStruct((n_rows, d), x.dtype),
        compiler_params=_params("arbitrary"),
        name="gather_rows",
    )(row_tok, blk_cnt, x)


BLOCK_VALID = 1
BLOCK_NEW_EXPERT = 2
WEIGHT_DMA_PRIORITY = 1
WEIGHT_DMA_SPLIT = 4


def _stream_expert_weights(plan, w_hbms, wbuf_ref, wbf_ref, sem, tile):
    be_ref, fl_ref, nx_ref, seg_ref, nseg_ref = plan
    p = pl.program_id(0)
    b = pl.program_id(1)
    n_seg = nseg_ref[0]
    g = p * n_seg + seg_ref[b]
    slot = jnp.bitwise_and(g, 1)

    def tile_copies(expert, col_pass, dst_slot):
        cols = pl.ds(pl.multiple_of(col_pass * tile, tile), tile)
        copies = []
        for k, w in enumerate(w_hbms):
            chunk = w.shape[1] // WEIGHT_DMA_SPLIT
            for c in range(WEIGHT_DMA_SPLIT):
                rows = pl.ds(c * chunk, chunk)
                copies.append(pltpu.make_async_copy(w.at[expert, rows, cols], wbuf_ref.at[dst_slot, k, rows],
                                                    sem.at[dst_slot, k]))
        return copies

    @pl.when(fl_ref[b] >= BLOCK_NEW_EXPERT)
    def _():
        @pl.when(g == 0)
        def _():
            for cp in tile_copies(be_ref[b], p, slot):
                cp.start(priority=WEIGHT_DMA_PRIORITY)

        for cp in tile_copies(be_ref[b], p, slot):
            cp.wait()

        @pl.when(g + 1 < pl.num_programs(0) * n_seg)
        def _():
            next_pass = jnp.where(seg_ref[b] + 1 == n_seg, p + 1, p)
            for cp in tile_copies(nx_ref[b], next_pass, 1 - slot):
                cp.start(priority=WEIGHT_DMA_PRIORITY)

        for k in range(len(w_hbms)):
            wbf_ref[k] = wbuf_ref[slot, k].astype(BF16)


def _expert_up_kernel(be_ref, fl_ref, nx_ref, seg_ref, nseg_ref, x_ref, wg_hbm, wu_hbm, o_ref, wbuf_ref, wbf_ref, sem):
    _stream_expert_weights((be_ref, fl_ref, nx_ref, seg_ref, nseg_ref), (wg_hbm, wu_hbm), wbuf_ref, wbf_ref, sem,
                           o_ref.shape[1])
    valid = jnp.bitwise_and(fl_ref[pl.program_id(1)], BLOCK_VALID) != 0

    @pl.when(valid)
    def _():
        x_hi, x_lo = [v.astype(BF16) for v in _unpack_bf16_pairs(x_ref[...])]
        half = x_hi.shape[1]
        dot = functools.partial(jnp.dot, preferred_element_type=F32)
        g = dot(x_hi, wbf_ref[0, 0:half, :]) + dot(x_lo, wbf_ref[0, half:2 * half, :])
        u = dot(x_hi, wbf_ref[1, 0:half, :]) + dot(x_lo, wbf_ref[1, half:2 * half, :])
        o_ref[...] = (_silu(g) * u).astype(o_ref.dtype)

    @pl.when(jnp.logical_not(valid))
    def _():
        o_ref[...] = jnp.zeros(o_ref.shape, o_ref.dtype)


def _expert_up(xs, plan, w_gate, w_up):
    n_rows = xs.shape[0]
    d, ff = w_gate.shape[1], w_gate.shape[2]
    tf = _pick(ff, (512, 256, 128))
    return pl.pallas_call(
        _expert_up_kernel,
        grid_spec=pltpu.PrefetchScalarGridSpec(
            num_scalar_prefetch=5,
            grid=(ff // tf, n_rows // EXPERT_BLOCK),
            in_specs=[pl.BlockSpec((EXPERT_BLOCK, d // 2), lambda f, b, *_: (b, 0)),
                      pl.BlockSpec(memory_space=pl.ANY), pl.BlockSpec(memory_space=pl.ANY)],
            out_specs=pl.BlockSpec((EXPERT_BLOCK, tf), lambda f, b, *_: (b, f)),
            scratch_shapes=[pltpu.VMEM((2, 2, d, tf), F32), pltpu.VMEM((2, d, tf), BF16),
                            pltpu.SemaphoreType.DMA((2, 2))]),
        out_shape=jax.ShapeDtypeStruct((n_rows, ff), BF16),
        compiler_params=_params("arbitrary", "arbitrary"),
        name="expert_up",
    )(*plan, xs, w_gate, w_up)


def _down_tile(d):
    return _pick(d, (2048, 1024, 512, 256))


def _expert_down_kernel(be_ref, fl_ref, nx_ref, seg_ref, nseg_ref, h_ref, wd_hbm, o_ref, wbuf_ref, wbf_ref, sem):
    tn = wbf_ref.shape[2]
    _stream_expert_weights((be_ref, fl_ref, nx_ref, seg_ref, nseg_ref), (wd_hbm,), wbuf_ref, wbf_ref, sem, tn)
    valid = jnp.bitwise_and(fl_ref[pl.program_id(1)], BLOCK_VALID) != 0

    @pl.when(valid)
    def _():
        y = jnp.dot(h_ref[...], wbf_ref[0], preferred_element_type=F32)
        o_ref[...] = _pack_bf16_pairs(y[:, :tn // 2], y[:, tn // 2:])

    @pl.when(jnp.logical_not(valid))
    def _():
        o_ref[...] = jnp.zeros(o_ref.shape, o_ref.dtype)


def _expert_down(hb, plan, w_down):
    n_rows, ff = hb.shape
    d = w_down.shape[2]
    tn = _down_tile(d)
    return pl.pallas_call(
        _expert_down_kernel,
        grid_spec=pltpu.PrefetchScalarGridSpec(
            num_scalar_prefetch=5,
            grid=(d // tn, n_rows // EXPERT_BLOCK),
            in_specs=[pl.BlockSpec((EXPERT_BLOCK, ff), lambda n, b, *_: (b, 0)),
                      pl.BlockSpec(memory_space=pl.ANY)],
            out_specs=pl.BlockSpec((EXPERT_BLOCK, tn // 2), lambda n, b, *_: (b, n)),
            scratch_shapes=[pltpu.VMEM((2, 1, ff, tn), F32), pltpu.VMEM((1, ff, tn), BF16),
                            pltpu.SemaphoreType.DMA((2, 1))]),
        out_shape=jax.ShapeDtypeStruct((n_rows, d // 2), jnp.uint32),
        compiler_params=_params("arbitrary", "arbitrary"),
        name="expert_down",
    )(*plan, hb, w_down)


def _combine_ln_kernel(d0_ref, d1_ref, ys_hbm, h_ref, gate_ref, g_ref, b_ref, op_ref, os_ref, buf_ref, sem,
                       *, alpha, prompt_blocks, pack_tile):
    tm = h_ref.shape[0]
    i = pl.program_id(0)
    base = i * tm

    def row_copy(k, r, src_row):
        return pltpu.make_async_copy(ys_hbm.at[pl.ds(src_row, 1), :], buf_ref.at[k, pl.ds(r, 1), :], sem)

    def start(r, carry):
        row_copy(0, r, d0_ref[base + r]).start(priority=0)
        row_copy(1, r, d1_ref[base + r]).start(priority=1)
        return carry

    def wait(r, carry):
        row_copy(0, r, 0).wait()
        row_copy(1, r, 0).wait()
        return carry

    lax.fori_loop(0, tm, start, 0)
    lax.fori_loop(0, tm, wait, 0)
    g0, g1 = gate_ref[:, 0:1], gate_ref[:, 1:2]
    pieces = []
    for n in range(h_ref.shape[1] // pack_tile):
        cols = slice(n * pack_tile // 2, (n + 1) * pack_tile // 2)
        hi0, lo0 = _unpack_bf16_pairs(buf_ref[0, :, cols])
        hi1, lo1 = _unpack_bf16_pairs(buf_ref[1, :, cols])
        pieces += [hi0 * g0 + hi1 * g1, lo0 * g0 + lo1 * g1]
    ffn = jnp.concatenate(pieces, axis=1)
    y = _ln_rows(alpha * h_ref[...] + ffn, g_ref[...], b_ref[...])

    @pl.when(i < prompt_blocks)
    def _():
        op_ref[...] = y

    @pl.when(i >= prompt_blocks)
    def _():
        os_ref[...] = y


def _combine_ln(ys, dest0, dest1, h, gate, g, b, alpha, n_prompt):
    n, d = h.shape
    tm = _pick(math.gcd(n_prompt, n - n_prompt), (128, 64, 8))
    pb = n_prompt // tm
    row = pl.BlockSpec((tm, d), lambda i, d0, d1: (i, 0))
    vec = pl.BlockSpec((1, d), lambda i, d0, d1: (0, 0))
    return pl.pallas_call(
        functools.partial(_combine_ln_kernel, alpha=alpha, prompt_blocks=pb, pack_tile=_down_tile(d)),
        grid_spec=pltpu.PrefetchScalarGridSpec(
            num_scalar_prefetch=2,
            grid=(n // tm,),
            in_specs=[pl.BlockSpec(memory_space=pl.ANY), row,
                      pl.BlockSpec((tm, LANES), lambda i, d0, d1: (i, 0)), vec, vec],
            out_specs=[pl.BlockSpec((tm, d), lambda i, d0, d1: (jnp.minimum(i, pb - 1), 0)),
                       pl.BlockSpec((tm, d), lambda i, d0, d1: (jnp.maximum(i - pb, 0), 0))],
            scratch_shapes=[pltpu.VMEM((2, tm, d // 2), jnp.uint32), pltpu.SemaphoreType.DMA(())]),
        out_shape=[jax.ShapeDtypeStruct((n_prompt, d), F32), jax.ShapeDtypeStruct((n - n_prompt, d), F32)],
        compiler_params=_params("arbitrary"),
        name="combine_ln",
    )(dest0, dest1, ys, h, gate, g.reshape(1, d), b.reshape(1, d))


def _dispatch_plan(eid, n_experts):
    n, k = eid.shape
    flat_e = eid.reshape(-1)
    onehot = (flat_e[:, None] == jnp.arange(n_experts, dtype=jnp.int32)[None, :]).astype(jnp.int32)
    csum = jnp.cumsum(onehot, axis=0)
    rank = jnp.sum(csum * onehot, axis=1) - 1
    counts = csum[-1]
    padded = (counts + EXPERT_BLOCK - 1) // EXPERT_BLOCK * EXPERT_BLOCK
    p_end = jnp.cumsum(padded)
    p_start = p_end - padded
    dest = (p_start[flat_e] + rank).astype(jnp.int32)
    n_blocks = -(-(n * k + n_experts * (EXPERT_BLOCK - 1)) // EXPERT_BLOCK)
    flat_tok = jnp.repeat(jnp.arange(n, dtype=jnp.int32), k)
    row_tok = jnp.zeros((n_blocks * EXPERT_BLOCK,), jnp.int32).at[dest].set(flat_tok, unique_indices=True)
    blk_row = jnp.arange(n_blocks, dtype=jnp.int32) * EXPERT_BLOCK
    blk_e = jnp.minimum(jnp.searchsorted(p_end, blk_row, side='right'), n_experts - 1).astype(jnp.int32)
    valid = blk_row < p_end[-1]
    blk_e = jnp.where(valid, blk_e, blk_e[p_end[-1] // EXPERT_BLOCK - 1])
    blk_cnt = jnp.clip(counts[blk_e] - (blk_row - p_start[blk_e]), 0, EXPERT_BLOCK).astype(jnp.int32)
    new_e = jnp.concatenate([jnp.ones((1,), bool), blk_e[1:] != blk_e[:-1]])
    flags = (valid.astype(jnp.int32) * BLOCK_VALID + new_e.astype(jnp.int32) * BLOCK_NEW_EXPERT)
    seg = jnp.cumsum(new_e.astype(jnp.int32)) - 1
    n_seg = seg[-1:] + 1
    seg_e = jnp.zeros((n_experts,), jnp.int32).at[seg].set(blk_e)
    next_e = seg_e[(seg + 1) % n_seg]
    plan = (blk_e, flags, next_e.astype(jnp.int32), seg.astype(jnp.int32), n_seg.astype(jnp.int32))
    return row_tok, blk_cnt, plan, dest.reshape(n, k)


def _pad_rows(a, rows):
    return jnp.pad(a, ((0, 0), (0, rows - a.shape[1]), (0, 0)))


def _layer(x_p, x_s, mem_prompt, win_k, win_v, conv_state, mem_k_s, mem_v_s,
           w_in, conv_w, conv_b, conv_ln_g, conv_ln_b, w_out, ln1_g, ln1_b,
           w_mem_q, w_mem_k, w_mem_v, w_mem_o, ln2_g, ln2_b,
           w_rg, b_rg, w_re, b_re, w_gate, w_up, w_down, ln3_g, ln3_b, alpha):
    bp, t, d = x_p.shape
    bs, ts, _ = x_s.shape
    past, heads = win_k.shape[1], win_k.shape[2]
    width = heads * HEAD_DIM
    conv_ch = conv_state.shape[2]
    cs = conv_ch // LANES
    mem_tokens, mem_heads = mem_k_s.shape[1], mem_k_s.shape[2]
    n_experts = w_gate.shape[0]
    assert bp == 1 and t % W_MAX == 0 and past == W_MAX and win_k.shape[3] == HEAD_DIM
    assert ts <= SAMPLE_U_PAD and w_in.shape[1] == 3 * width + 2 * conv_ch and width + conv_ch == d
    n_s = bs * ts

    x = jnp.concatenate([x_p.reshape(t, d), x_s.reshape(n_s, d)], axis=0)
    proj3 = _matmul_slabs(x, w_in.astype(BF16))

    def sample_part(lo, hi):
        return jnp.transpose(proj3[lo:hi, t:], (1, 0, 2)).reshape(bs, ts, (hi - lo) * LANES)

    q_s = sample_part(0, heads)
    k_s = sample_part(heads, 2 * heads)
    v_s = sample_part(2 * heads, 3 * heads)
    ga_s = sample_part(3 * heads, 3 * heads + cs)
    gb_s = sample_part(3 * heads + cs, 3 * heads + 2 * cs)

    put_sample = lambda full, rows: lax.dynamic_update_slice(full, rows.astype(full.dtype), (t,) + (0,) * (full.ndim - 1))
    attn_p, attn_s, win_k_s, win_v_s = _attn(proj3, t, heads, _pad_rows(q_s, SAMPLE_Q_PAD),
                                             k_s.reshape(bs, ts, heads, HEAD_DIM),
                                             v_s.reshape(bs, ts, heads, HEAD_DIM), win_k, win_v)
    attn = put_sample(attn_p, attn_s[:, :ts].reshape(n_s, width))
    keep_p = min(W_MAX, t)
    win_k_p = jnp.transpose(proj3[heads:2 * heads, t - keep_p:t], (1, 0, 2))[None]
    win_v_p = jnp.transpose(proj3[2 * heads:3 * heads, t - keep_p:t], (1, 0, 2))[None]

    conv_p, u_last = _conv_prompt(proj3, t, 3 * heads, 3 * heads + cs, cs, conv_w, conv_b)
    conv_state_p = u_last[CONV_HALO - (CONV_W - 1):][None]
    conv_s, conv_state_s = _conv_sample(conv_state, _pad_rows(ga_s, SAMPLE_U_PAD), _pad_rows(gb_s, SAMPLE_U_PAD),
                                        conv_w, conv_b, ts)
    conv_all = put_sample(conv_p, conv_s[:, :ts].reshape(n_s, conv_ch))
    c_all = _ln_silu(conv_all, conv_ln_g, conv_ln_b)

    h1, h1_bf = _res_ln(x, _matmul([attn, c_all], w_out.astype(BF16)), ln1_g, ln1_b, alpha)

    qm = _matmul([h1_bf], w_mem_q.astype(BF16), out_dtype=BF16)
    mem_x = mem_prompt.reshape(bp * mem_tokens, d)
    mem_k_p = _matmul([mem_x], w_mem_k.astype(BF16))
    mem_v_p = _matmul([mem_x], w_mem_v.astype(BF16))
    om_s = _mem_attn(_pad_rows(qm[t:].reshape(bs, ts, d), SAMPLE_Q_PAD), mem_k_s.reshape(bs, mem_tokens, d),
                     mem_v_s.reshape(bs, mem_tokens, d), mem_heads)[:, :ts]
    om = put_sample(_mem_attn(qm[None], mem_k_p[None], mem_v_p[None], mem_heads)[0], om_s.reshape(n_s, d))
    h2, h2_packed = _res_ln(h1, _matmul([om], w_mem_o.astype(BF16)), ln2_g, ln2_b, alpha, pack=True)

    eid, gate = _router(h2, w_rg, b_rg, w_re, b_re)
    row_tok, blk_cnt, plan, dest = _dispatch_plan(eid, n_experts)
    xs = _gather_rows(h2_packed, row_tok, blk_cnt)
    hb = _expert_up(xs, plan, w_gate, w_up)
    ys = _expert_down(hb, plan, w_down)
    y_p, y_s = _combine_ln(ys, dest[:, 0], dest[:, 1], h2, gate, ln3_g, ln3_b, alpha, t)

    mem_shape = (bp, mem_tokens, mem_heads, d // mem_heads)
    return (y_p.reshape(bp, t, d), y_s.reshape(bs, ts, d), win_k_p, win_v_p, conv_state_p,
            mem_k_p.reshape(mem_shape), mem_v_p.reshape(mem_shape), win_k_s, win_v_s, conv_state_s)


def kernel(x_prompt, x_sample, mem_prompt, cache_win_k, cache_win_v, state_conv, cache_mem_k, cache_mem_v, w_in, conv_w, conv_b, conv_ln_g, conv_ln_b, w_out, ln1_g, ln1_b, w_mem_q, w_mem_k, w_mem_v, w_mem_o, ln2_g, ln2_b, w_router_group, b_router_group, w_router_expert, b_router_expert, w_exp_gate, w_exp_up, w_exp_down, ln3_g, ln3_b):
    depth = w_in.shape[0]
    alpha = (2 * depth) ** 0.25
    hp, hs = x_prompt, x_sample
    per_layer = []
    for l in range(depth):
        outs = _layer(hp, hs, mem_prompt, cache_win_k[l], cache_win_v[l], state_conv[l], cache_mem_k[l], cache_mem_v[l],
                      w_in[l], conv_w[l], conv_b[l], conv_ln_g[l], conv_ln_b[l], w_out[l], ln1_g[l], ln1_b[l],
                      w_mem_q[l], w_mem_k[l], w_mem_v[l], w_mem_o[l], ln2_g[l], ln2_b[l],
                      w_router_group[l], b_router_group[l], w_router_expert[l], b_router_expert[l],
                      w_exp_gate[l], w_exp_up[l], w_exp_down[l], ln3_g[l], ln3_b[l], alpha)
        hp, hs = outs[0], outs[1]
        per_layer.append(outs[2:])
    stacked = [jnp.stack([layer[i] for layer in per_layer]) for i in range(8)]
    return (hp, hs, *stacked)
```

```python
import functools
import math

import jax
import jax.numpy as jnp
from jax import lax
from jax.experimental import pallas as pl
from jax.experimental.pallas import tpu as pltpu

F32 = jnp.float32
BF16 = jnp.bfloat16

LANES = 128
HEAD_DIM = 128
DILATED_BRANCHES = ((128, 1), (512, 4), (2048, 16))
W_MAX = 2048
KEYS_PER_BLOCK = 128
ATTN_GROUP = 8
CONV_W = 31
CONV_HALO = 32
EXPERT_BLOCK = 128
LN_EPS = 1e-5
VMEM_LIMIT = 56 * 1024 * 1024


def _params(*sem):
    return pltpu.CompilerParams(dimension_semantics=sem, vmem_limit_bytes=VMEM_LIMIT)


def _pick(n, candidates):
    for c in candidates:
        if n % c == 0:
            return c
    return n


def _mm_kernel(*refs):
    a_refs, b_ref, o_ref = refs[:-2], refs[-2], refs[-1]
    acc = None
    k0 = 0
    for a_ref in a_refs:
        k1 = k0 + a_ref.shape[1]
        part = jnp.dot(a_ref[...].astype(BF16), b_ref[k0:k1, :].astype(BF16), preferred_element_type=F32)
        acc = part if acc is None else acc + part
        k0 = k1
    o_ref[...] = acc.astype(o_ref.dtype)


def _matmul(srcs, b, out_dtype=F32):
    m = srcs[0].shape[0]
    k, n = b.shape
    assert sum(a.shape[1] for a in srcs) == k
    tm = _pick(m, (640, 512, 256, 128))
    tn = _pick(n, (1024, 512, 256, 128))
    return pl.pallas_call(
        _mm_kernel,
        grid=(m // tm, n // tn),
        in_specs=[pl.BlockSpec((tm, a.shape[1]), lambda i, j: (i, 0)) for a in srcs]
        + [pl.BlockSpec((k, tn), lambda i, j: (0, j))],
        out_specs=pl.BlockSpec((tm, tn), lambda i, j: (i, j)),
        out_shape=jax.ShapeDtypeStruct((m, n), out_dtype),
        compiler_params=_params("parallel", "arbitrary"),
        name="matmul",
    )(*srcs, b)


def _mm_slab_kernel(a_ref, b_ref, o_ref):
    r = jnp.dot(a_ref[...].astype(BF16), b_ref[...].astype(BF16), preferred_element_type=F32)
    for s in range(o_ref.shape[0]):
        o_ref[s] = r[:, s * LANES:(s + 1) * LANES]


def _matmul_slabs(a, b):
    m, k = a.shape
    n = b.shape[1]
    tm = _pick(m, (640, 512, 256, 128))
    tn = _pick(n, (1024, 512, 256, 128))
    return pl.pallas_call(
        _mm_slab_kernel,
        grid=(m // tm, n // tn),
        in_specs=[pl.BlockSpec((tm, k), lambda i, j: (i, 0)),
                  pl.BlockSpec((k, tn), lambda i, j: (0, j))],
        out_specs=pl.BlockSpec((tn // LANES, tm, LANES), lambda i, j: (j, i, 0)),
        out_shape=jax.ShapeDtypeStruct((n // LANES, m, LANES), F32),
        compiler_params=_params("parallel", "arbitrary"),
        name="matmul_slabs",
    )(a, b)


def _ln_rows(x, g, b):
    mu = jnp.mean(x, axis=-1, keepdims=True)
    xc = x - mu
    var = jnp.mean(xc * xc, axis=-1, keepdims=True)
    return xc * lax.rsqrt(var + LN_EPS) * g + b


def _silu(x):
    return x * jax.nn.sigmoid(x)


def _pack_bf16_pairs(hi, lo):
    hi = lax.bitcast_convert_type(hi.astype(BF16).astype(F32), jnp.uint32)
    lo = lax.bitcast_convert_type(lo.astype(BF16).astype(F32), jnp.uint32)
    return jnp.bitwise_or(hi, jnp.right_shift(lo, jnp.uint32(16)))


def _unpack_bf16_pairs(u):
    hi = lax.bitcast_convert_type(jnp.bitwise_and(u, jnp.uint32(0xFFFF0000)), F32)
    lo = lax.bitcast_convert_type(jnp.left_shift(u, jnp.uint32(16)), F32)
    return hi, lo


def _res_ln_kernel(res_ref, t_ref, g_ref, b_ref, o_ref, o2_ref, *, alpha, pack):
    y = _ln_rows(alpha * res_ref[...] + t_ref[...], g_ref[...], b_ref[...])
    o_ref[...] = y
    half = y.shape[1] // 2
    o2_ref[...] = _pack_bf16_pairs(y[:, :half], y[:, half:]) if pack else y.astype(BF16)


def _res_ln(res, t, g, b, alpha, pack=False):
    n, d = res.shape
    tm = _pick(n, (128, 64, 8))
    row = pl.BlockSpec((tm, d), lambda i: (i, 0))
    vec = pl.BlockSpec((1, d), lambda i: (0, 0))
    second = (jax.ShapeDtypeStruct((n, d // 2), jnp.uint32), pl.BlockSpec((tm, d // 2), lambda i: (i, 0))) if pack \
        else (jax.ShapeDtypeStruct((n, d), BF16), row)
    return pl.pallas_call(
        functools.partial(_res_ln_kernel, alpha=alpha, pack=pack),
        grid=(n // tm,),
        in_specs=[row, row, vec, vec],
        out_specs=[row, second[1]],
        out_shape=[jax.ShapeDtypeStruct((n, d), F32), second[0]],
        compiler_params=_params("parallel"),
        name="res_ln",
    )(res, t, g.reshape(1, d), b.reshape(1, d))


def _ln_silu_kernel(x_ref, g_ref, b_ref, o_ref):
    o_ref[...] = _silu(_ln_rows(x_ref[...], g_ref[...], b_ref[...])).astype(o_ref.dtype)


def _ln_silu(x, g, b):
    n, d = x.shape
    tm = _pick(n, (256, 128, 64, 8))
    row = pl.BlockSpec((tm, d), lambda i: (i, 0))
    vec = pl.BlockSpec((1, d), lambda i: (0, 0))
    return pl.pallas_call(
        _ln_silu_kernel,
        grid=(n // tm,),
        in_specs=[row, vec, vec],
        out_specs=row,
        out_shape=jax.ShapeDtypeStruct((n, d), BF16),
        compiler_params=_params("parallel"),
        name="ln_silu",
    )(x, g.reshape(1, d), b.reshape(1, d))


def _attn_prompt_body(c, n_chunks, q_ref, kp_ref, kc_ref, vp_ref, vc_ref, o_ref, acc_ref, m_ref, l_ref):
    scale = 1.0 / math.sqrt(HEAD_DIM)
    blk = KEYS_PER_BLOCK
    ii = lax.broadcasted_iota(jnp.int32, (blk, blk), 0)
    jj = lax.broadcasted_iota(jnp.int32, (blk, blk), 1)
    nt = (((1,), (1,)), ((), ()))

    def rows(start, d):
        return pl.ds(start, blk) if d == 1 else pl.ds(start, blk, stride=d)

    def sub_blocks(subs, d, first, last):
        n = range(len(subs))
        cur = [rows(r + d * blk * b, d) for r, b in subs]
        prev = [rows(r + W_MAX - d * blk, d) if b == 0 else rows(r + d * blk * (b - 1), d) for r, b in subs]
        k_prev = [kp_ref if b == 0 else kc_ref for _, b in subs]
        v_prev = [vp_ref if b == 0 else vc_ref for _, b in subs]
        q = [(q_ref[cur[i], :] * scale).astype(BF16) for i in n]
        s_a = [lax.dot_general(q[i], k_prev[i][prev[i], :].astype(BF16), nt, preferred_element_type=F32) for i in n]
        s_b = [lax.dot_general(q[i], kc_ref[cur[i], :].astype(BF16), nt, preferred_element_type=F32) for i in n]
        for i, (_, b) in enumerate(subs):
            mask_a = jj >= ii
            if b == 0:
                mask_a = jnp.logical_and(mask_a, c > 0)
            s_a[i] = jnp.where(mask_a, s_a[i], -jnp.inf)
            s_b[i] = jnp.where(jj <= ii, s_b[i], -jnp.inf)
        m_loc = [jnp.maximum(jnp.max(s_a[i], axis=1, keepdims=True), jnp.max(s_b[i], axis=1, keepdims=True))
                 for i in n]
        p_a = [jnp.exp(s_a[i] - m_loc[i]) for i in n]
        p_b = [jnp.exp(s_b[i] - m_loc[i]) for i in n]
        l_loc = [jnp.sum(p_a[i], axis=1, keepdims=True) + jnp.sum(p_b[i], axis=1, keepdims=True) for i in n]
        acc_loc = [jnp.dot(p_a[i].astype(BF16), v_prev[i][prev[i], :].astype(BF16), preferred_element_type=F32)
                   + jnp.dot(p_b[i].astype(BF16), vc_ref[cur[i], :].astype(BF16), preferred_element_type=F32)
                   for i in n]
        for i in n:
            if first:
                m_new, l_new, acc_new = m_loc[i], l_loc[i], acc_loc[i]
            else:
                m_old = m_ref[cur[i], :][:, :1]
                l_old = l_ref[cur[i], :][:, :1]
                m_new = jnp.maximum(m_old, m_loc[i])
                a_old = jnp.exp(m_old - m_new)
                a_loc = jnp.exp(m_loc[i] - m_new)
                l_new = a_old * l_old + a_loc * l_loc[i]
                acc_new = a_old * acc_ref[cur[i], :] + a_loc * acc_loc[i]
            if last:
                o_ref[cur[i], :] = (acc_new / l_new).astype(o_ref.dtype)
            else:
                m_ref[cur[i], :] = jnp.broadcast_to(m_new, (blk, LANES))
                l_ref[cur[i], :] = jnp.broadcast_to(l_new, (blk, LANES))
                acc_ref[cur[i], :] = acc_new

    @pl.when(c < n_chunks)
    def _():
        order = sorted(DILATED_BRANCHES, key=lambda wd: -wd[1])
        for idx, (w, d) in enumerate(order):
            subs = [(r, b) for b in range(W_MAX // (d * blk)) for r in range(d)]
            for g in range(0, len(subs), ATTN_GROUP):
                sub_blocks(subs[g:g + ATTN_GROUP], d, idx == 0, idx == len(order) - 1)

    @pl.when(c >= n_chunks)
    def _():
        o_ref[...] = jnp.zeros(o_ref.shape, o_ref.dtype)


SAMPLE_Q_PAD = 8
SAMPLE_HEADS = 8


def _key_multiplicity(dist):
    mult = jnp.zeros(dist.shape, F32)
    for (w, d) in DILATED_BRANCHES:
        hit = jnp.logical_and(jnp.bitwise_and(dist, d - 1) == 0, dist <= w)
        mult = mult + jnp.where(hit, 1.0, 0.0)
    return jnp.where(dist >= 0, mult, 0.0)


def _window_copies(b, g, kn_ref, vn_ref, kc_ref, vc_ref, wk_hbm, wv_hbm, sem):
    _, past, hg, _ = kc_ref.shape
    t_new = kn_ref.shape[1]
    heads = pl.ds(pl.multiple_of(g * hg, hg), hg)
    copies = []
    for k, (c_ref, n_ref, w_hbm) in enumerate(((kc_ref, kn_ref, wk_hbm), (vc_ref, vn_ref, wv_hbm))):
        copies.append(pltpu.make_async_copy(c_ref.at[0, pl.ds(t_new, past - t_new)],
                                            w_hbm.at[b, pl.ds(0, past - t_new), heads, :], sem.at[2 * k]))
        copies.append(pltpu.make_async_copy(n_ref.at[0], w_hbm.at[b, pl.ds(past - t_new, t_new), heads, :],
                                            sem.at[2 * k + 1]))
    return copies


def _attn_sample_body(q_ref, kn_ref, vn_ref, kc_ref, vc_ref, o_ref):
    _, past, hg, _ = kc_ref.shape
    t_new = kn_ref.shape[1]
    scale = 1.0 / math.sqrt(HEAD_DIM)
    qi = lax.broadcasted_iota(jnp.int32, (SAMPLE_Q_PAD, past), 0)
    rho = lax.broadcasted_iota(jnp.int32, (SAMPLE_Q_PAD, past), 1)
    mult_c = _key_multiplicity(past + qi - rho)
    qi1 = lax.broadcasted_iota(jnp.int32, (SAMPLE_Q_PAD, 1), 0)
    mult_n = [_key_multiplicity(qi1 - i) for i in range(t_new)]
    nt = (((1,), (1,)), ((), ()))
    k_rows = kc_ref.at[0].reshape(past * hg, HEAD_DIM)
    v_rows = vc_ref.at[0].reshape(past * hg, HEAD_DIM)
    for h in range(hg):
        sl = slice(h * HEAD_DIM, (h + 1) * HEAD_DIM)
        head_rows = pl.ds(h, past, stride=hg)
        q = q_ref[0, :, sl] * scale
        s_c = lax.dot_general(q.astype(BF16), k_rows[head_rows, :].astype(BF16), nt, preferred_element_type=F32)
        s_c = jnp.where(mult_c > 0, s_c, -jnp.inf)
        s_n = [jnp.sum(q * kn_ref[0, i, h:h + 1, :], axis=1, keepdims=True) for i in range(t_new)]
        s_n = [jnp.where(mult_n[i] > 0, s_n[i], -jnp.inf) for i in range(t_new)]
        m = jnp.max(s_c, axis=1, keepdims=True)
        for i in range(t_new):
            m = jnp.maximum(m, s_n[i])
        p_c = mult_c * jnp.exp(s_c - m)
        l = jnp.sum(p_c, axis=1, keepdims=True)
        o = jnp.dot(p_c.astype(BF16), v_rows[head_rows, :].astype(BF16), preferred_element_type=F32)
        for i in range(t_new):
            p_n = mult_n[i] * jnp.exp(s_n[i] - m)
            l = l + p_n
            o = o + p_n * vn_ref[0, i, h:h + 1, :]
        o_ref[0, :, sl] = o / l


def _attn_kernel(*refs, n_chunks, heads, prompt_steps, sample_steps, sample_groups):
    prompt_in, sample_in = refs[0:5], refs[5:10]
    o_ref, so_ref, wk_hbm, wv_hbm, acc_ref, m_ref, l_ref, sem = refs[10:]
    s = pl.program_id(0)
    window_copies = functools.partial(_window_copies, s // sample_groups, s % sample_groups, *sample_in[1:],
                                      wk_hbm, wv_hbm, sem)

    @pl.when(s < sample_steps)
    def _():
        for cp in window_copies():
            cp.start()
        _attn_sample_body(*sample_in, so_ref)

    @pl.when(s < prompt_steps)
    def _():
        _attn_prompt_body(s // heads, n_chunks, *prompt_in, o_ref, acc_ref, m_ref, l_ref)

    @pl.when(s < sample_steps)
    def _():
        for cp in window_copies():
            cp.wait()


def _attn(proj3, t, heads, q, k_new, v_new, k_cache, v_cache):
    n_total = proj3.shape[1]
    n_chunks = t // W_MAX
    bsz, past, _, _ = k_cache.shape
    t_new = k_new.shape[1]
    hg = SAMPLE_HEADS if heads % SAMPLE_HEADS == 0 else heads
    groups = heads // hg
    prompt_steps = pl.cdiv(n_total, W_MAX) * heads
    sample_steps = bsz * groups

    ps = lambda s: jnp.minimum(s, prompt_steps - 1)
    chunk = lambda s: jnp.minimum(ps(s) // heads, n_chunks - 1)
    blk = (None, W_MAX, LANES)
    cur = lambda off: pl.BlockSpec(blk, lambda s: (off + ps(s) % heads, chunk(s), 0))
    prev = lambda off: pl.BlockSpec(blk, lambda s: (off + ps(s) % heads, jnp.maximum(chunk(s) - 1, 0), 0))
    ss = lambda s: jnp.minimum(s, sample_steps - 1)
    qspec = pl.BlockSpec((1, SAMPLE_Q_PAD, hg * HEAD_DIM), lambda s: (ss(s) // groups, 0, ss(s) % groups))
    kv = lambda rows: pl.BlockSpec((1, rows, hg, HEAD_DIM), lambda s: (ss(s) // groups, 0, ss(s) % groups, 0))
    win = jax.ShapeDtypeStruct(k_cache.shape, F32)
    return pl.pallas_call(
        functools.partial(_attn_kernel, n_chunks=n_chunks, heads=heads, prompt_steps=prompt_steps,
                          sample_steps=sample_steps, sample_groups=groups),
        grid=(max(prompt_steps, sample_steps),),
        in_specs=[cur(0), prev(heads), cur(heads), prev(2 * heads), cur(2 * heads),
                  qspec, kv(t_new), kv(t_new), kv(past), kv(past)],
        out_specs=[pl.BlockSpec((W_MAX, LANES), lambda s: (ps(s) // heads, ps(s) % heads)), qspec,
                   pl.BlockSpec(memory_space=pl.ANY), pl.BlockSpec(memory_space=pl.ANY)],
        out_shape=[jax.ShapeDtypeStruct((n_total, heads * LANES), BF16),
                   jax.ShapeDtypeStruct((bsz, SAMPLE_Q_PAD, heads * HEAD_DIM), F32), win, win],
        scratch_shapes=[pltpu.VMEM((W_MAX, LANES), F32)] * 3 + [pltpu.SemaphoreType.DMA((4,))],
        compiler_params=_params("arbitrary"),
        name="attn",
    )(proj3, proj3, proj3, proj3, proj3, q, k_new, v_new, k_cache, v_cache)


CONV_ROWS = 64


def _conv_taps(ubuf_ref, w_ref, cb_ref, out_ref, n_rows, base):
    for r0 in range(0, n_rows, CONV_ROWS):
        nr = min(CONV_ROWS, n_rows - r0)
        acc = jnp.broadcast_to(cb_ref[...], (nr, LANES))
        for j in range(CONV_W):
            acc = acc + ubuf_ref[pl.ds(base + r0 + j, nr), :] * w_ref[j:j + 1, :]
        out_ref[pl.ds(r0, nr), :] = acc


def _conv_prompt_kernel(ga_ref, gb_ref, w_ref, cb_ref, o_ref, ulast_ref, ubuf_ref, *, n_blocks):
    i = pl.program_id(1)
    tb = ga_ref.shape[0]

    @pl.when(i == 0)
    def _():
        ubuf_ref[0:CONV_HALO, :] = jnp.zeros((CONV_HALO, LANES), F32)

    @pl.when(jnp.logical_and(i > 0, i < n_blocks))
    def _():
        ubuf_ref[0:CONV_HALO, :] = ubuf_ref[tb:tb + CONV_HALO, :]

    @pl.when(i < n_blocks)
    def _():
        ubuf_ref[CONV_HALO:CONV_HALO + tb, :] = ga_ref[...] * jax.nn.sigmoid(gb_ref[...])
        _conv_taps(ubuf_ref, w_ref, cb_ref, o_ref, tb, CONV_HALO - (CONV_W - 1))

    @pl.when(i == n_blocks - 1)
    def _():
        ulast_ref[...] = ubuf_ref[tb:tb + CONV_HALO, :]

    @pl.when(i >= n_blocks)
    def _():
        o_ref[...] = jnp.zeros(o_ref.shape, o_ref.dtype)


def _conv_prompt(proj3, t, slab_a, slab_b, n_slabs, conv_w, conv_b):
    n_total = proj3.shape[1]
    tb = _pick(t, (512, 256, 128))
    c = n_slabs * LANES
    n_blocks = t // tb
    blk = lambda i: jnp.minimum(i, n_blocks - 1)
    return pl.pallas_call(
        functools.partial(_conv_prompt_kernel, n_blocks=n_blocks),
        grid=(n_slabs, pl.cdiv(n_total, tb)),
        in_specs=[pl.BlockSpec((None, tb, LANES), lambda s, i: (slab_a + s, blk(i), 0)),
                  pl.BlockSpec((None, tb, LANES), lambda s, i: (slab_b + s, blk(i), 0)),
                  pl.BlockSpec((CONV_W, LANES), lambda s, i: (0, s)),
                  pl.BlockSpec((1, LANES), lambda s, i: (0, s))],
        out_specs=[pl.BlockSpec((tb, LANES), lambda s, i: (i, s)),
                   pl.BlockSpec((CONV_HALO, LANES), lambda s, i: (0, s))],
        out_shape=[jax.ShapeDtypeStruct((n_total, c), F32), jax.ShapeDtypeStruct((CONV_HALO, c), F32)],
        scratch_shapes=[pltpu.VMEM((CONV_HALO + tb, LANES), F32)],
        compiler_params=_params("parallel", "arbitrary"),
        name="conv_prompt",
    )(proj3, proj3, conv_w, conv_b.reshape(1, c))


SAMPLE_U_PAD = 8


def _conv_sample_kernel(st_ref, ga_ref, gb_ref, w_ref, cb_ref, o_ref, nst_ref, ubuf_ref, *, t_new):
    bsz, hist, _ = st_ref.shape

    def one_sequence(b, carry):
        ubuf_ref[0:hist, :] = st_ref[b]
        ubuf_ref[hist:hist + SAMPLE_U_PAD, :] = ga_ref[b] * jax.nn.sigmoid(gb_ref[b])
        _conv_taps(ubuf_ref, w_ref, cb_ref, o_ref.at[b], SAMPLE_U_PAD, 0)
        nst_ref[b] = ubuf_ref[pl.ds(t_new, hist), :]
        return carry

    lax.fori_loop(0, bsz, one_sequence, 0)


def _conv_sample(state, ga, gb, conv_w, conv_b, t_new):
    bsz, hist, c = state.shape
    assert hist == CONV_W - 1 and t_new <= SAMPLE_U_PAD
    spec = lambda rows: pl.BlockSpec((bsz, rows, LANES), lambda s: (0, 0, s))
    return pl.pallas_call(
        functools.partial(_conv_sample_kernel, t_new=t_new),
        grid=(c // LANES,),
        in_specs=[spec(hist), spec(SAMPLE_U_PAD), spec(SAMPLE_U_PAD),
                  pl.BlockSpec((CONV_W, LANES), lambda s: (0, s)),
                  pl.BlockSpec((1, LANES), lambda s: (0, s))],
        out_specs=[spec(SAMPLE_U_PAD), spec(hist)],
        out_shape=[jax.ShapeDtypeStruct((bsz, SAMPLE_U_PAD, c), F32),
                   jax.ShapeDtypeStruct((bsz, hist, c), F32)],
        scratch_shapes=[pltpu.VMEM((hist + SAMPLE_U_PAD + 2, LANES), F32)],
        compiler_params=_params("parallel"),
        name="conv_sample",
    )(state, ga, gb, conv_w, conv_b.reshape(1, c))


def _mem_attn_kernel(q_ref, k_ref, v_ref, o_ref):
    hd = q_ref.shape[-1]
    nt = (((1,), (1,)), ((), ()))
    s = lax.dot_general(q_ref[...].astype(BF16), k_ref[...].astype(BF16), nt,
                        preferred_element_type=F32) * (1.0 / math.sqrt(hd))
    m = jnp.max(s, axis=1, keepdims=True)
    p = jnp.exp(s - m)
    l = jnp.sum(p, axis=1, keepdims=True)
    o = jnp.dot(p.astype(BF16), v_ref[...].astype(BF16), preferred_element_type=F32)
    o_ref[...] = (o / l).astype(o_ref.dtype)


def _mem_attn(q, k, v, heads):
    bsz, tq_all, d = q.shape
    m = k.shape[1]
    hd = d // heads
    tq = 512 if tq_all >= 512 else tq_all
    qspec = pl.BlockSpec((None, tq, hd), lambda b, h, i: (b, i, h))
    kspec = pl.BlockSpec((None, m, hd), lambda b, h, i: (b, 0, h))
    return pl.pallas_call(
        _mem_attn_kernel,
        grid=(bsz, heads, pl.cdiv(tq_all, tq)),
        in_specs=[qspec, kspec, kspec],
        out_specs=qspec,
        out_shape=jax.ShapeDtypeStruct((bsz, tq_all, d), BF16),
        compiler_params=_params("parallel", "parallel", "arbitrary"),
        name="mem_attn",
    )(q, k, v)


def _router_kernel(x_ref, w_ref, b_ref, eid_ref, gate_ref, *, n_groups, epg):
    logits = jnp.dot(x_ref[...].astype(BF16), w_ref[...].astype(BF16), preferred_element_type=F32) + b_ref[...]
    lane = lax.broadcasted_iota(jnp.int32, logits.shape, 1).astype(F32)
    neg = -jnp.inf
    none = float(LANES)

    def first_max(vals):
        top = jnp.max(vals, axis=1, keepdims=True)
        idx = jnp.min(jnp.where(vals == top, lane, none), axis=1, keepdims=True)
        return top, idx

    g_logits = jnp.where(lane < n_groups, logits, neg)
    g_top, grp = first_max(g_logits)
    p_grp = 1.0 / jnp.sum(jnp.exp(g_logits - g_top), axis=1, keepdims=True)
    lo = n_groups + grp * epg
    e_logits = jnp.where(jnp.logical_and(lane >= lo, lane < lo + epg), logits, neg)
    v1, i1 = first_max(e_logits)
    v2, i2 = first_max(jnp.where(lane == i1, neg, e_logits))
    e21 = jnp.exp(v2 - v1)
    g1 = p_grp * (1.0 / (1.0 + e21))
    g2 = p_grp * (e21 / (1.0 + e21))
    eid = jnp.where(lane == 0.0, i1 - n_groups, jnp.where(lane == 1.0, i2 - n_groups, 0.0))
    eid_ref[...] = eid.astype(jnp.int32)
    gate_ref[...] = jnp.where(lane == 0.0, g1, jnp.where(lane == 1.0, g2, 0.0))


def _router(x, w_rg, b_rg, w_re, b_re):
    n, d = x.shape
    n_groups, epg = w_re.shape[1], w_re.shape[2]
    n_log = n_groups + n_groups * epg
    assert n_log <= LANES
    w = jnp.concatenate([w_rg, w_re.reshape(d, n_groups * epg), jnp.zeros((d, LANES - n_log), F32)], axis=1)
    b = jnp.concatenate([b_rg, b_re.reshape(-1), jnp.zeros((LANES - n_log,), F32)]).reshape(1, LANES)
    tm = _pick(n, (640, 512, 256, 128))
    row = pl.BlockSpec((tm, LANES), lambda i: (i, 0))
    eid, gate = pl.pallas_call(
        functools.partial(_router_kernel, n_groups=n_groups, epg=epg),
        grid=(n // tm,),
        in_specs=[pl.BlockSpec((tm, d), lambda i: (i, 0)),
                  pl.BlockSpec((d, LANES), lambda i: (0, 0)),
                  pl.BlockSpec((1, LANES), lambda i: (0, 0))],
        out_specs=[row, row],
        out_shape=[jax.ShapeDtypeStruct((n, LANES), jnp.int32), jax.ShapeDtypeStruct((n, LANES), F32)],
        compiler_params=_params("parallel"),
        name="router",
    )(x, w, b)
    return eid[:, :2], gate


def _gather_rows_kernel(tok_ref, cnt_ref, x_hbm, o_ref, sem):
    b = pl.program_id(0)
    base = b * EXPERT_BLOCK
    cnt = cnt_ref[b]

    def row_copy(r, src_row):
        return pltpu.make_async_copy(x_hbm.at[pl.ds(src_row, 1), :], o_ref.at[pl.ds(r, 1), :], sem)

    def start_pair(i, carry):
        row_copy(2 * i, tok_ref[base + 2 * i]).start(priority=0)
        row_copy(2 * i + 1, tok_ref[base + 2 * i + 1]).start(priority=1)
        return carry

    def wait(r, carry):
        row_copy(r, 0).wait()
        return carry

    def zero_row(r, carry):
        o_ref[pl.ds(r, 1), :] = jnp.zeros((1, o_ref.shape[1]), o_ref.dtype)
        return carry

    lax.fori_loop(0, jnp.right_shift(cnt, 1), start_pair, 0)

    @pl.when(jnp.bitwise_and(cnt, 1) == 1)
    def _():
        row_copy(cnt - 1, tok_ref[base + cnt - 1]).start()

    lax.fori_loop(cnt, EXPERT_BLOCK, zero_row, 0)
    lax.fori_loop(0, cnt, wait, 0)


def _gather_rows(x, row_tok, blk_cnt):
    n_rows = row_tok.shape[0]
    d = x.shape[1]
    return pl.pallas_call(
        _gather_rows_kernel,
        grid_spec=pltpu.PrefetchScalarGridSpec(
            num_scalar_prefetch=2,
            grid=(n_rows // EXPERT_BLOCK,),
            in_specs=[pl.BlockSpec(memory_space=pl.ANY)],
            out_specs=pl.BlockSpec((EXPERT_BLOCK, d), lambda b, tok, cnt: (b, 0)),
            scratch_shapes=[pltpu.SemaphoreType.DMA(())]),
        out_shape=jax.ShapeDtypeStruct((n_rows, d), x.dtype),
        compiler_params=_params("arbitrary"),
        name="gather_rows",
    )(row_tok, blk_cnt, x)


BLOCK_VALID = 1
BLOCK_NEW_EXPERT = 2
WEIGHT_DMA_PRIORITY = 1
WEIGHT_DMA_SPLIT = 4


def _stream_expert_weights(plan, w_hbms, wbuf_ref, wbf_ref, sem, tile):
    be_ref, fl_ref, nx_ref, seg_ref, nseg_ref = plan
    p = pl.program_id(0)
    b = pl.program_id(1)
    n_seg = nseg_ref[0]
    g = p * n_seg + seg_ref[b]
    slot = jnp.bitwise_and(g, 1)

    def tile_copies(expert, col_pass, dst_slot):
        cols = pl.ds(pl.multiple_of(col_pass * tile, tile), tile)
        copies = []
        for k, w in enumerate(w_hbms):
            chunk = w.shape[1] // WEIGHT_DMA_SPLIT
            for c in range(WEIGHT_DMA_SPLIT):
                rows = pl.ds(c * chunk, chunk)
                copies.append(pltpu.make_async_copy(w.at[expert, rows, cols], wbuf_ref.at[dst_slot, k, rows],
                                                    sem.at[dst_slot, k]))
        return copies

    @pl.when(fl_ref[b] >= BLOCK_NEW_EXPERT)
    def _():
        @pl.when(g == 0)
        def _():
            for cp in tile_copies(be_ref[b], p, slot):
                cp.start(priority=WEIGHT_DMA_PRIORITY)

        @pl.when(g + 1 < pl.num_programs(0) * n_seg)
        def _():
            next_pass = jnp.where(seg_ref[b] + 1 == n_seg, p + 1, p)
            for cp in tile_copies(nx_ref[b], next_pass, 1 - slot):
                cp.start(priority=WEIGHT_DMA_PRIORITY)

        for cp in tile_copies(be_ref[b], p, slot):
            cp.wait()

        for k in range(len(w_hbms)):
            wbf_ref[k] = wbuf_ref[slot, k].astype(BF16)


def _expert_up_kernel(be_ref, fl_ref, nx_ref, seg_ref, nseg_ref, x_ref, wg_hbm, wu_hbm, o_ref, wbuf_ref, wbf_ref, sem):
    _stream_expert_weights((be_ref, fl_ref, nx_ref, seg_ref, nseg_ref), (wg_hbm, wu_hbm), wbuf_ref, wbf_ref, sem,
                           o_ref.shape[1])
    valid = jnp.bitwise_and(fl_ref[pl.program_id(1)], BLOCK_VALID) != 0

    @pl.when(valid)
    def _():
        x_hi, x_lo = [v.astype(BF16) for v in _unpack_bf16_pairs(x_ref[...])]
        half = x_hi.shape[1]
        dot = functools.partial(jnp.dot, preferred_element_type=F32)
        g = dot(x_hi, wbf_ref[0, 0:half, :]) + dot(x_lo, wbf_ref[0, half:2 * half, :])
        u = dot(x_hi, wbf_ref[1, 0:half, :]) + dot(x_lo, wbf_ref[1, half:2 * half, :])
        o_ref[...] = (_silu(g) * u).astype(o_ref.dtype)

    @pl.when(jnp.logical_not(valid))
    def _():
        o_ref[...] = jnp.zeros(o_ref.shape, o_ref.dtype)


def _expert_up(xs, plan, w_gate, w_up):
    n_rows = xs.shape[0]
    d, ff = w_gate.shape[1], w_gate.shape[2]
    tf = _pick(ff, (512, 256, 128))
    return pl.pallas_call(
        _expert_up_kernel,
        grid_spec=pltpu.PrefetchScalarGridSpec(
            num_scalar_prefetch=5,
            grid=(ff // tf, n_rows // EXPERT_BLOCK),
            in_specs=[pl.BlockSpec((EXPERT_BLOCK, d // 2), lambda f, b, *_: (b, 0)),
                      pl.BlockSpec(memory_space=pl.ANY), pl.BlockSpec(memory_space=pl.ANY)],
            out_specs=pl.BlockSpec((EXPERT_BLOCK, tf), lambda f, b, *_: (b, f)),
            scratch_shapes=[pltpu.VMEM((2, 2, d, tf), F32), pltpu.VMEM((2, d, tf), BF16),
                            pltpu.SemaphoreType.DMA((2, 2))]),
        out_shape=jax.ShapeDtypeStruct((n_rows, ff), BF16),
        compiler_params=_params("arbitrary", "arbitrary"),
        name="expert_up",
    )(*plan, xs, w_gate, w_up)


def _down_tile(d):
    return _pick(d, (2048, 1024, 512, 256))


def _expert_down_kernel(be_ref, fl_ref, nx_ref, seg_ref, nseg_ref, h_ref, wd_hbm, o_ref, wbuf_ref, wbf_ref, sem):
    tn = wbf_ref.shape[2]
    _stream_expert_weights((be_ref, fl_ref, nx_ref, seg_ref, nseg_ref), (wd_hbm,), wbuf_ref, wbf_ref, sem, tn)
    valid = jnp.bitwise_and(fl_ref[pl.program_id(1)], BLOCK_VALID) != 0

    @pl.when(valid)
    def _():
        y = jnp.dot(h_ref[...], wbf_ref[0], preferred_element_type=F32)
        o_ref[...] = _pack_bf16_pairs(y[:, :tn // 2], y[:, tn // 2:])

    @pl.when(jnp.logical_not(valid))
    def _():
        o_ref[...] = jnp.zeros(o_ref.shape, o_ref.dtype)


def _expert_down(hb, plan, w_down):
    n_rows, ff = hb.shape
    d = w_down.shape[2]
    tn = _down_tile(d)
    return pl.pallas_call(
        _expert_down_kernel,
        grid_spec=pltpu.PrefetchScalarGridSpec(
            num_scalar_prefetch=5,
            grid=(d // tn, n_rows // EXPERT_BLOCK),
            in_specs=[pl.BlockSpec((EXPERT_BLOCK, ff), lambda n, b, *_: (b, 0)),
                      pl.BlockSpec(memory_space=pl.ANY)],
            out_specs=pl.BlockSpec((EXPERT_BLOCK, tn // 2), lambda n, b, *_: (b, n)),
            scratch_shapes=[pltpu.VMEM((2, 1, ff, tn), F32), pltpu.VMEM((1, ff, tn), BF16),
                            pltpu.SemaphoreType.DMA((2, 1))]),
        out_shape=jax.ShapeDtypeStruct((n_rows, d // 2), jnp.uint32),
        compiler_params=_params("arbitrary", "arbitrary"),
        name="expert_down",
    )(*plan, hb, w_down)


def _combine_ln_kernel(d0_ref, d1_ref, ys_hbm, h_ref, gate_ref, g_ref, b_ref, op_ref, os_ref, buf_ref, sem,
                       *, alpha, prompt_blocks, pack_tile):
    tm = h_ref.shape[0]
    i = pl.program_id(0)
    base = i * tm

    def row_copy(k, r, src_row):
        return pltpu.make_async_copy(ys_hbm.at[pl.ds(src_row, 1), :], buf_ref.at[k, pl.ds(r, 1), :], sem)

    def start(r, carry):
        row_copy(0, r, d0_ref[base + r]).start(priority=0)
        row_copy(1, r, d1_ref[base + r]).start(priority=1)
        return carry

    def wait(r, carry):
        row_copy(0, r, 0).wait()
        row_copy(1, r, 0).wait()
        return carry

    lax.fori_loop(0, tm, start, 0)
    lax.fori_loop(0, tm, wait, 0)
    g0, g1 = gate_ref[:, 0:1], gate_ref[:, 1:2]
    pieces = []
    for n in range(h_ref.shape[1] // pack_tile):
        cols = slice(n * pack_tile // 2, (n + 1) * pack_tile // 2)
        hi0, lo0 = _unpack_bf16_pairs(buf_ref[0, :, cols])
        hi1, lo1 = _unpack_bf16_pairs(buf_ref[1, :, cols])
        pieces += [hi0 * g0 + hi1 * g1, lo0 * g0 + lo1 * g1]
    ffn = jnp.concatenate(pieces, axis=1)
    y = _ln_rows(alpha * h_ref[...] + ffn, g_ref[...], b_ref[...])

    @pl.when(i < prompt_blocks)
    def _():
        op_ref[...] = y

    @pl.when(i >= prompt_blocks)
    def _():
        os_ref[...] = y


def _combine_ln(ys, dest0, dest1, h, gate, g, b, alpha, n_prompt):
    n, d = h.shape
    tm = _pick(math.gcd(n_prompt, n - n_prompt), (128, 64, 8))
    pb = n_prompt // tm
    row = pl.BlockSpec((tm, d), lambda i, d0, d1: (i, 0))
    vec = pl.BlockSpec((1, d), lambda i, d0, d1: (0, 0))
    return pl.pallas_call(
        functools.partial(_combine_ln_kernel, alpha=alpha, prompt_blocks=pb, pack_tile=_down_tile(d)),
        grid_spec=pltpu.PrefetchScalarGridSpec(
            num_scalar_prefetch=2,
            grid=(n // tm,),
            in_specs=[pl.BlockSpec(memory_space=pl.ANY), row,
                      pl.BlockSpec((tm, LANES), lambda i, d0, d1: (i, 0)), vec, vec],
            out_specs=[pl.BlockSpec((tm, d), lambda i, d0, d1: (jnp.minimum(i, pb - 1), 0)),
                       pl.BlockSpec((tm, d), lambda i, d0, d1: (jnp.maximum(i - pb, 0), 0))],
            scratch_shapes=[pltpu.VMEM((2, tm, d // 2), jnp.uint32), pltpu.SemaphoreType.DMA(())]),
        out_shape=[jax.ShapeDtypeStruct((n_prompt, d), F32), jax.ShapeDtypeStruct((n - n_prompt, d), F32)],
        compiler_params=_params("arbitrary"),
        name="combine_ln",
    )(dest0, dest1, ys, h, gate, g.reshape(1, d), b.reshape(1, d))


def _dispatch_plan(eid, n_experts):
    n, k = eid.shape
    flat_e = eid.reshape(-1)
    onehot = (flat_e[:, None] == jnp.arange(n_experts, dtype=jnp.int32)[None, :]).astype(jnp.int32)
    csum = jnp.cumsum(onehot, axis=0)
    rank = jnp.sum(csum * onehot, axis=1) - 1
    counts = csum[-1]
    padded = (counts + EXPERT_BLOCK - 1) // EXPERT_BLOCK * EXPERT_BLOCK
    p_end = jnp.cumsum(padded)
    p_start = p_end - padded
    dest = (p_start[flat_e] + rank).astype(jnp.int32)
    n_blocks = -(-(n * k + n_experts * (EXPERT_BLOCK - 1)) // EXPERT_BLOCK)
    flat_tok = jnp.repeat(jnp.arange(n, dtype=jnp.int32), k)
    row_tok = jnp.zeros((n_blocks * EXPERT_BLOCK,), jnp.int32).at[dest].set(flat_tok, unique_indices=True)
    blk_row = jnp.arange(n_blocks, dtype=jnp.int32) * EXPERT_BLOCK
    blk_e = jnp.minimum(jnp.searchsorted(p_end, blk_row, side='right'), n_experts - 1).astype(jnp.int32)
    valid = blk_row < p_end[-1]
    blk_e = jnp.where(valid, blk_e, blk_e[p_end[-1] // EXPERT_BLOCK - 1])
    blk_cnt = jnp.clip(counts[blk_e] - (blk_row - p_start[blk_e]), 0, EXPERT_BLOCK).astype(jnp.int32)
    new_e = jnp.concatenate([jnp.ones((1,), bool), blk_e[1:] != blk_e[:-1]])
    flags = (valid.astype(jnp.int32) * BLOCK_VALID + new_e.astype(jnp.int32) * BLOCK_NEW_EXPERT)
    seg = jnp.cumsum(new_e.astype(jnp.int32)) - 1
    n_seg = seg[-1:] + 1
    seg_e = jnp.zeros((n_experts,), jnp.int32).at[seg].set(blk_e)
    next_e = seg_e[(seg + 1) % n_seg]
    plan = (blk_e, flags, next_e.astype(jnp.int32), seg.astype(jnp.int32), n_seg.astype(jnp.int32))
    return row_tok, blk_cnt, plan, dest.reshape(n, k)


def _pad_rows(a, rows):
    return jnp.pad(a, ((0, 0), (0, rows - a.shape[1]), (0, 0)))


def _layer(x_p, x_s, mem_prompt, win_k, win_v, conv_state, mem_k_s, mem_v_s,
           w_in, conv_w, conv_b, conv_ln_g, conv_ln_b, w_out, ln1_g, ln1_b,
           w_mem_q, w_mem_k, w_mem_v, w_mem_o, ln2_g, ln2_b,
           w_rg, b_rg, w_re, b_re, w_gate, w_up, w_down, ln3_g, ln3_b, alpha):
    bp, t, d = x_p.shape
    bs, ts, _ = x_s.shape
    past, heads = win_k.shape[1], win_k.shape[2]
    width = heads * HEAD_DIM
    conv_ch = conv_state.shape[2]
    cs = conv_ch // LANES
    mem_tokens, mem_heads = mem_k_s.shape[1], mem_k_s.shape[2]
    n_experts = w_gate.shape[0]
    assert bp == 1 and t % W_MAX == 0 and past == W_MAX and win_k.shape[3] == HEAD_DIM
    assert ts <= SAMPLE_U_PAD and w_in.shape[1] == 3 * width + 2 * conv_ch and width + conv_ch == d
    n_s = bs * ts

    x = jnp.concatenate([x_p.reshape(t, d), x_s.reshape(n_s, d)], axis=0)
    proj3 = _matmul_slabs(x, w_in.astype(BF16))

    def sample_part(lo, hi):
        return jnp.transpose(proj3[lo:hi, t:], (1, 0, 2)).reshape(bs, ts, (hi - lo) * LANES)

    q_s = sample_part(0, heads)
    k_s = sample_part(heads, 2 * heads)
    v_s = sample_part(2 * heads, 3 * heads)
    ga_s = sample_part(3 * heads, 3 * heads + cs)
    gb_s = sample_part(3 * heads + cs, 3 * heads + 2 * cs)

    put_sample = lambda full, rows: lax.dynamic_update_slice(full, rows.astype(full.dtype), (t,) + (0,) * (full.ndim - 1))
    attn_p, attn_s, win_k_s, win_v_s = _attn(proj3, t, heads, _pad_rows(q_s, SAMPLE_Q_PAD),
                                             k_s.reshape(bs, ts, heads, HEAD_DIM),
                                             v_s.reshape(bs, ts, heads, HEAD_DIM), win_k, win_v)
    attn = put_sample(attn_p, attn_s[:, :ts].reshape(n_s, width))
    keep_p = min(W_MAX, t)
    win_k_p = jnp.transpose(proj3[heads:2 * heads, t - keep_p:t], (1, 0, 2))[None]
    win_v_p = jnp.transpose(proj3[2 * heads:3 * heads, t - keep_p:t], (1, 0, 2))[None]

    conv_p, u_last = _conv_prompt(proj3, t, 3 * heads, 3 * heads + cs, cs, conv_w, conv_b)
    conv_state_p = u_last[CONV_HALO - (CONV_W - 1):][None]
    conv_s, conv_state_s = _conv_sample(conv_state, _pad_rows(ga_s, SAMPLE_U_PAD), _pad_rows(gb_s, SAMPLE_U_PAD),
                                        conv_w, conv_b, ts)
    conv_all = put_sample(conv_p, conv_s[:, :ts].reshape(n_s, conv_ch))
    c_all = _ln_silu(conv_all, conv_ln_g, conv_ln_b)

    h1, h1_bf = _res_ln(x, _matmul([attn, c_all], w_out.astype(BF16)), ln1_g, ln1_b, alpha)

    qm = _matmul([h1_bf], w_mem_q.astype(BF16), out_dtype=BF16)
    mem_x = mem_prompt.reshape(bp * mem_tokens, d)
    mem_k_p = _matmul([mem_x], w_mem_k.astype(BF16))
    mem_v_p = _matmul([mem_x], w_mem_v.astype(BF16))
    om_s = _mem_attn(_pad_rows(qm[t:].reshape(bs, ts, d), SAMPLE_Q_PAD), mem_k_s.reshape(bs, mem_tokens, d),
                     mem_v_s.reshape(bs, mem_tokens, d), mem_heads)[:, :ts]
    om = put_sample(_mem_attn(qm[None], mem_k_p[None], mem_v_p[None], mem_heads)[0], om_s.reshape(n_s, d))
    h2, h2_packed = _res_ln(h1, _matmul([om], w_mem_o.astype(BF16)), ln2_g, ln2_b, alpha, pack=True)

    eid, gate = _router(h2, w_rg, b_rg, w_re, b_re)
    row_tok, blk_cnt, plan, dest = _dispatch_plan(eid, n_experts)
    xs = _gather_rows(h2_packed, row_tok, blk_cnt)
    hb = _expert_up(xs, plan, w_gate, w_up)
    ys = _expert_down(hb, plan, w_down)
    y_p, y_s = _combine_ln(ys, dest[:, 0], dest[:, 1], h2, gate, ln3_g, ln3_b, alpha, t)

    mem_shape = (bp, mem_tokens, mem_heads, d // mem_heads)
    return (y_p.reshape(bp, t, d), y_s.reshape(bs, ts, d), win_k_p, win_v_p, conv_state_p,
            mem_k_p.reshape(mem_shape), mem_v_p.reshape(mem_shape), win_k_s, win_v_s, conv_state_s)


def kernel(x_prompt, x_sample, mem_prompt, cache_win_k, cache_win_v, state_conv, cache_mem_k, cache_mem_v, w_in, conv_w, conv_b, conv_ln_g, conv_ln_b, w_out, ln1_g, ln1_b, w_mem_q, w_mem_k, w_mem_v, w_mem_o, ln2_g, ln2_b, w_router_group, b_router_group, w_router_expert, b_router_expert, w_exp_gate, w_exp_up, w_exp_down, ln3_g, ln3_b):
    depth = w_in.shape[0]
    alpha = (2 * depth) ** 0.25
    hp, hs = x_prompt, x_sample
    per_layer = []
    for l in range(depth):
        outs = _layer(hp, hs, mem_prompt, cache_win_k[l], cache_win_v[l], state_conv[l], cache_mem_k[l], cache_mem_v[l],
                      w_in[l], conv_w[l], conv_b[l], conv_ln_g[l], conv_ln_b[l], w_out[l], ln1_g[l], ln1_b[l],
                      w_mem_q[l], w_mem_k[l], w_mem_v[l], w_mem_o[l], ln2_g[l], ln2_b[l],
                      w_router_group[l], b_router_group[l], w_router_expert[l], b_router_expert[l],
                      w_exp_gate[l], w_exp_up[l], w_exp_down[l], ln3_g[l], ln3_b[l], alpha)
        hp, hs = outs[0], outs[1]
        per_layer.append(outs[2:])
    stacked = [jnp.stack([layer[i] for layer in per_layer]) for i in range(8)]
    return (hp, hs, *stacked)
```

```python
import functools
import math

import jax
import jax.numpy as jnp
from jax import lax
from jax.experimental import pallas as pl
from jax.experimental.pallas import tpu as pltpu

F32 = jnp.float32
BF16 = jnp.bfloat16

LANES = 128
HEAD_DIM = 128
DILATED_BRANCHES = ((128, 1), (512, 4), (2048, 16))
W_MAX = 2048
KEYS_PER_BLOCK = 128
ATTN_GROUP = 8
CONV_W = 31
CONV_HALO = 32
EXPERT_BLOCK = 128
LN_EPS = 1e-5
VMEM_LIMIT = 56 * 1024 * 1024


def _params(*sem):
    return pltpu.CompilerParams(dimension_semantics=sem, vmem_limit_bytes=VMEM_LIMIT)


def _pick(n, candidates):
    for c in candidates:
        if n % c == 0:
            return c
    return n


def _mm_kernel(*refs):
    a_refs, b_ref, o_ref = refs[:-2], refs[-2], refs[-1]
    acc = None
    k0 = 0
    for a_ref in a_refs:
        k1 = k0 + a_ref.shape[1]
        part = jnp.dot(a_ref[...].astype(BF16), b_ref[k0:k1, :].astype(BF16), preferred_element_type=F32)
        acc = part if acc is None else acc + part
        k0 = k1
    o_ref[...] = acc.astype(o_ref.dtype)


def _matmul(srcs, b, out_dtype=F32):
    m = srcs[0].shape[0]
    k, n = b.shape
    assert sum(a.shape[1] for a in srcs) == k
    tm = _pick(m, (640, 512, 256, 128))
    tn = _pick(n, (1024, 512, 256, 128))
    return pl.pallas_call(
        _mm_kernel,
        grid=(m // tm, n // tn),
        in_specs=[pl.BlockSpec((tm, a.shape[1]), lambda i, j: (i, 0)) for a in srcs]
        + [pl.BlockSpec((k, tn), lambda i, j: (0, j))],
        out_specs=pl.BlockSpec((tm, tn), lambda i, j: (i, j)),
        out_shape=jax.ShapeDtypeStruct((m, n), out_dtype),
        compiler_params=_params("parallel", "arbitrary"),
        name="matmul",
    )(*srcs, b)


def _mm_slab_kernel(a_ref, b_ref, o_ref):
    r = jnp.dot(a_ref[...].astype(BF16), b_ref[...].astype(BF16), preferred_element_type=F32)
    for s in range(o_ref.shape[0]):
        o_ref[s] = r[:, s * LANES:(s + 1) * LANES]


def _matmul_slabs(a, b):
    m, k = a.shape
    n = b.shape[1]
    tm = _pick(m, (640, 512, 256, 128))
    tn = _pick(n, (1024, 512, 256, 128))
    return pl.pallas_call(
        _mm_slab_kernel,
        grid=(m // tm, n // tn),
        in_specs=[pl.BlockSpec((tm, k), lambda i, j: (i, 0)),
                  pl.BlockSpec((k, tn), lambda i, j: (0, j))],
        out_specs=pl.BlockSpec((tn // LANES, tm, LANES), lambda i, j: (j, i, 0)),
        out_shape=jax.ShapeDtypeStruct((n // LANES, m, LANES), F32),
        compiler_params=_params("parallel", "arbitrary"),
        name="matmul_slabs",
    )(a, b)


def _ln_rows(x, g, b):
    mu = jnp.mean(x, axis=-1, keepdims=True)
    xc = x - mu
    var = jnp.mean(xc * xc, axis=-1, keepdims=True)
    return xc * lax.rsqrt(var + LN_EPS) * g + b


def _silu(x):
    return x * jax.nn.sigmoid(x)


def _pack_bf16_pairs(hi, lo):
    hi = lax.bitcast_convert_type(hi.astype(BF16).astype(F32), jnp.uint32)
    lo = lax.bitcast_convert_type(lo.astype(BF16).astype(F32), jnp.uint32)
    return jnp.bitwise_or(hi, jnp.right_shift(lo, jnp.uint32(16)))


def _unpack_bf16_pairs(u):
    hi = lax.bitcast_convert_type(jnp.bitwise_and(u, jnp.uint32(0xFFFF0000)), F32)
    lo = lax.bitcast_convert_type(jnp.left_shift(u, jnp.uint32(16)), F32)
    return hi, lo


def _res_ln_kernel(res_ref, t_ref, g_ref, b_ref, o_ref, o2_ref, *, alpha, pack):
    y = _ln_rows(alpha * res_ref[...] + t_ref[...], g_ref[...], b_ref[...])
    o_ref[...] = y
    half = y.shape[1] // 2
    o2_ref[...] = _pack_bf16_pairs(y[:, :half], y[:, half:]) if pack else y.astype(BF16)


def _res_ln(res, t, g, b, alpha, pack=False):
    n, d = res.shape
    tm = _pick(n, (128, 64, 8))
    row = pl.BlockSpec((tm, d), lambda i: (i, 0))
    vec = pl.BlockSpec((1, d), lambda i: (0, 0))
    second = (jax.ShapeDtypeStruct((n, d // 2), jnp.uint32), pl.BlockSpec((tm, d // 2), lambda i: (i, 0))) if pack \
        else (jax.ShapeDtypeStruct((n, d), BF16), row)
    return pl.pallas_call(
        functools.partial(_res_ln_kernel, alpha=alpha, pack=pack),
        grid=(n // tm,),
        in_specs=[row, row, vec, vec],
        out_specs=[row, second[1]],
        out_shape=[jax.ShapeDtypeStruct((n, d), F32), second[0]],
        compiler_params=_params("parallel"),
        name="res_ln",
    )(res, t, g.reshape(1, d), b.reshape(1, d))


def _ln_silu_kernel(x_ref, g_ref, b_ref, o_ref):
    o_ref[...] = _silu(_ln_rows(x_ref[...], g_ref[...], b_ref[...])).astype(o_ref.dtype)


def _ln_silu(x, g, b):
    n, d = x.shape
    tm = _pick(n, (256, 128, 64, 8))
    row = pl.BlockSpec((tm, d), lambda i: (i, 0))
    vec = pl.BlockSpec((1, d), lambda i: (0, 0))
    return pl.pallas_call(
        _ln_silu_kernel,
        grid=(n // tm,),
        in_specs=[row, vec, vec],
        out_specs=row,
        out_shape=jax.ShapeDtypeStruct((n, d), BF16),
        compiler_params=_params("parallel"),
        name="ln_silu",
    )(x, g.reshape(1, d), b.reshape(1, d))


def _attn_prompt_body(c, n_chunks, q_ref, kp_ref, kc_ref, vp_ref, vc_ref, o_ref, acc_ref, m_ref, l_ref):
    scale = 1.0 / math.sqrt(HEAD_DIM)
    blk = KEYS_PER_BLOCK
    ii = lax.broadcasted_iota(jnp.int32, (blk, blk), 0)
    jj = lax.broadcasted_iota(jnp.int32, (blk, blk), 1)
    nt = (((1,), (1,)), ((), ()))

    def rows(start, d):
        return pl.ds(start, blk) if d == 1 else pl.ds(start, blk, stride=d)

    def sub_blocks(subs, d, first, last):
        n = range(len(subs))
        cur = [rows(r + d * blk * b, d) for r, b in subs]
        prev = [rows(r + W_MAX - d * blk, d) if b == 0 else rows(r + d * blk * (b - 1), d) for r, b in subs]
        k_prev = [kp_ref if b == 0 else kc_ref for _, b in subs]
        v_prev = [vp_ref if b == 0 else vc_ref for _, b in subs]
        q = [(q_ref[cur[i], :] * scale).astype(BF16) for i in n]
        s_a = [lax.dot_general(q[i], k_prev[i][prev[i], :].astype(BF16), nt, preferred_element_type=F32) for i in n]
        s_b = [lax.dot_general(q[i], kc_ref[cur[i], :].astype(BF16), nt, preferred_element_type=F32) for i in n]
        for i, (_, b) in enumerate(subs):
            mask_a = jj >= ii
            if b == 0:
                mask_a = jnp.logical_and(mask_a, c > 0)
            s_a[i] = jnp.where(mask_a, s_a[i], -jnp.inf)
            s_b[i] = jnp.where(jj <= ii, s_b[i], -jnp.inf)
        m_loc = [jnp.maximum(jnp.max(s_a[i], axis=1, keepdims=True), jnp.max(s_b[i], axis=1, keepdims=True))
                 for i in n]
        p_a = [jnp.exp(s_a[i] - m_loc[i]) for i in n]
        p_b = [jnp.exp(s_b[i] - m_loc[i]) for i in n]
        l_loc = [jnp.sum(p_a[i], axis=1, keepdims=True) + jnp.sum(p_b[i], axis=1, keepdims=True) for i in n]
        acc_loc = [jnp.dot(p_a[i].astype(BF16), v_prev[i][prev[i], :].astype(BF16), preferred_element_type=F32)
                   + jnp.dot(p_b[i].astype(BF16), vc_ref[cur[i], :].astype(BF16), preferred_element_type=F32)
                   for i in n]
        for i in n:
            if first:
                m_new, l_new, acc_new = m_loc[i], l_loc[i], acc_loc[i]
            else:
                m_old = m_ref[cur[i], :][:, :1]
                l_old = l_ref[cur[i], :][:, :1]
                m_new = jnp.maximum(m_old, m_loc[i])
                a_old = jnp.exp(m_old - m_new)
                a_loc = jnp.exp(m_loc[i] - m_new)
                l_new = a_old * l_old + a_loc * l_loc[i]
                acc_new = a_old * acc_ref[cur[i], :] + a_loc * acc_loc[i]
            if last:
                o_ref[cur[i], :] = (acc_new / l_new).astype(o_ref.dtype)
            else:
                m_ref[cur[i], :] = jnp.broadcast_to(m_new, (blk, LANES))
                l_ref[cur[i], :] = jnp.broadcast_to(l_new, (blk, LANES))
                acc_ref[cur[i], :] = acc_new

    @pl.when(c < n_chunks)
    def _():
        order = sorted(DILATED_BRANCHES, key=lambda wd: -wd[1])
        for idx, (w, d) in enumerate(order):
            subs = [(r, b) for b in range(W_MAX // (d * blk)) for r in range(d)]
            for g in range(0, len(subs), ATTN_GROUP):
                sub_blocks(subs[g:g + ATTN_GROUP], d, idx == 0, idx == len(order) - 1)

    @pl.when(c >= n_chunks)
    def _():
        o_ref[...] = jnp.zeros(o_ref.shape, o_ref.dtype)


SAMPLE_Q_PAD = 8
SAMPLE_HEADS = 8


def _key_multiplicity(dist):
    mult = jnp.zeros(dist.shape, F32)
    for (w, d) in DILATED_BRANCHES:
        hit = jnp.logical_and(jnp.bitwise_and(dist, d - 1) == 0, dist <= w)
        mult = mult + jnp.where(hit, 1.0, 0.0)
    return jnp.where(dist >= 0, mult, 0.0)


def _window_copies(b, g, kn_ref, vn_ref, kc_ref, vc_ref, wk_hbm, wv_hbm, sem):
    _, past, hg, _ = kc_ref.shape
    t_new = kn_ref.shape[1]
    heads = pl.ds(pl.multiple_of(g * hg, hg), hg)
    copies = []
    for k, (c_ref, n_ref, w_hbm) in enumerate(((kc_ref, kn_ref, wk_hbm), (vc_ref, vn_ref, wv_hbm))):
        copies.append(pltpu.make_async_copy(c_ref.at[0, pl.ds(t_new, past - t_new)],
                                            w_hbm.at[b, pl.ds(0, past - t_new), heads, :], sem.at[2 * k]))
        copies.append(pltpu.make_async_copy(n_ref.at[0], w_hbm.at[b, pl.ds(past - t_new, t_new), heads, :],
                                            sem.at[2 * k + 1]))
    return copies


def _attn_sample_body(q_ref, kn_ref, vn_ref, kc_ref, vc_ref, o_ref):
    _, past, hg, _ = kc_ref.shape
    t_new = kn_ref.shape[1]
    scale = 1.0 / math.sqrt(HEAD_DIM)
    qi = lax.broadcasted_iota(jnp.int32, (SAMPLE_Q_PAD, past), 0)
    rho = lax.broadcasted_iota(jnp.int32, (SAMPLE_Q_PAD, past), 1)
    mult_c = _key_multiplicity(past + qi - rho)
    qi1 = lax.broadcasted_iota(jnp.int32, (SAMPLE_Q_PAD, 1), 0)
    mult_n = [_key_multiplicity(qi1 - i) for i in range(t_new)]
    nt = (((1,), (1,)), ((), ()))
    k_rows = kc_ref.at[0].reshape(past * hg, HEAD_DIM)
    v_rows = vc_ref.at[0].reshape(past * hg, HEAD_DIM)
    for h in range(hg):
        sl = slice(h * HEAD_DIM, (h + 1) * HEAD_DIM)
        head_rows = pl.ds(h, past, stride=hg)
        q = q_ref[0, :, sl] * scale
        s_c = lax.dot_general(q.astype(BF16), k_rows[head_rows, :].astype(BF16), nt, preferred_element_type=F32)
        s_c = jnp.where(mult_c > 0, s_c, -jnp.inf)
        s_n = [jnp.sum(q * kn_ref[0, i, h:h + 1, :], axis=1, keepdims=True) for i in range(t_new)]
        s_n = [jnp.where(mult_n[i] > 0, s_n[i], -jnp.inf) for i in range(t_new)]
        m = jnp.max(s_c, axis=1, keepdims=True)
        for i in range(t_new):
            m = jnp.maximum(m, s_n[i])
        p_c = mult_c * jnp.exp(s_c - m)
        l = jnp.sum(p_c, axis=1, keepdims=True)
        o = jnp.dot(p_c.astype(BF16), v_rows[head_rows, :].astype(BF16), preferred_element_type=F32)
        for i in range(t_new):
            p_n = mult_n[i] * jnp.exp(s_n[i] - m)
            l = l + p_n
            o = o + p_n * vn_ref[0, i, h:h + 1, :]
        o_ref[0, :, sl] = o / l


def _attn_kernel(*refs, n_chunks, heads, prompt_steps, sample_steps, sample_groups):
    prompt_in, sample_in = refs[0:5], refs[5:10]
    o_ref, so_ref, wk_hbm, wv_hbm, acc_ref, m_ref, l_ref, sem = refs[10:]
    s = pl.program_id(0)
    window_copies = functools.partial(_window_copies, s // sample_groups, s % sample_groups, *sample_in[1:],
                                      wk_hbm, wv_hbm, sem)

    @pl.when(s < sample_steps)
    def _():
        for cp in window_copies():
            cp.start()
        _attn_sample_body(*sample_in, so_ref)

    @pl.when(s < prompt_steps)
    def _():
        _attn_prompt_body(s // heads, n_chunks, *prompt_in, o_ref, acc_ref, m_ref, l_ref)

    @pl.when(s < sample_steps)
    def _():
        for cp in window_copies():
            cp.wait()


def _attn(proj3, t, heads, q, k_new, v_new, k_cache, v_cache):
    n_total = proj3.shape[1]
    n_chunks = t // W_MAX
    bsz, past, _, _ = k_cache.shape
    t_new = k_new.shape[1]
    hg = SAMPLE_HEADS if heads % SAMPLE_HEADS == 0 else heads
    groups = heads // hg
    prompt_steps = pl.cdiv(n_total, W_MAX) * heads
    sample_steps = bsz * groups

    ps = lambda s: jnp.minimum(s, prompt_steps - 1)
    chunk = lambda s: jnp.minimum(ps(s) // heads, n_chunks - 1)
    blk = (None, W_MAX, LANES)
    cur = lambda off: pl.BlockSpec(blk, lambda s: (off + ps(s) % heads, chunk(s), 0))
    prev = lambda off: pl.BlockSpec(blk, lambda s: (off + ps(s) % heads, jnp.maximum(chunk(s) - 1, 0), 0))
    ss = lambda s: jnp.minimum(s, sample_steps - 1)
    qspec = pl.BlockSpec((1, SAMPLE_Q_PAD, hg * HEAD_DIM), lambda s: (ss(s) // groups, 0, ss(s) % groups))
    kv = lambda rows: pl.BlockSpec((1, rows, hg, HEAD_DIM), lambda s: (ss(s) // groups, 0, ss(s) % groups, 0))
    win = jax.ShapeDtypeStruct(k_cache.shape, F32)
    return pl.pallas_call(
        functools.partial(_attn_kernel, n_chunks=n_chunks, heads=heads, prompt_steps=prompt_steps,
                          sample_steps=sample_steps, sample_groups=groups),
        grid=(max(prompt_steps, sample_steps),),
        in_specs=[cur(0), prev(heads), cur(heads), prev(2 * heads), cur(2 * heads),
                  qspec, kv(t_new), kv(t_new), kv(past), kv(past)],
        out_specs=[pl.BlockSpec((W_MAX, LANES), lambda s: (ps(s) // heads, ps(s) % heads)), qspec,
                   pl.BlockSpec(memory_space=pl.ANY), pl.BlockSpec(memory_space=pl.ANY)],
        out_shape=[jax.ShapeDtypeStruct((n_total, heads * LANES), BF16),
                   jax.ShapeDtypeStruct((bsz, SAMPLE_Q_PAD, heads * HEAD_DIM), F32), win, win],
        scratch_shapes=[pltpu.VMEM((W_MAX, LANES), F32)] * 3 + [pltpu.SemaphoreType.DMA((4,))],
        compiler_params=_params("arbitrary"),
        name="attn",
    )(proj3, proj3, proj3, proj3, proj3, q, k_new, v_new, k_cache, v_cache)


CONV_ROWS = 64


def _conv_taps(ubuf_ref, w_ref, cb_ref, out_ref, n_rows, base):
    for r0 in range(0, n_rows, CONV_ROWS):
        nr = min(CONV_ROWS, n_rows - r0)
        acc = jnp.broadcast_to(cb_ref[...], (nr, LANES))
        for j in range(CONV_W):
            acc = acc + ubuf_ref[pl.ds(base + r0 + j, nr), :] * w_ref[j:j + 1, :]
        out_ref[pl.ds(r0, nr), :] = acc


CONV_SLABS = 2


def _conv_prompt_kernel(ga_ref, gb_ref, w_ref, cb_ref, o_ref, ulast_ref, ubuf_ref, *, n_blocks):
    i = pl.program_id(1)
    n_slabs, tb, _ = ga_ref.shape
    for k in range(n_slabs):
        lanes = pl.ds(k * LANES, LANES)
        ubuf = ubuf_ref.at[k]

        @pl.when(i == 0)
        def _():
            ubuf[0:CONV_HALO, :] = jnp.zeros((CONV_HALO, LANES), F32)

        @pl.when(jnp.logical_and(i > 0, i < n_blocks))
        def _():
            ubuf[0:CONV_HALO, :] = ubuf[tb:tb + CONV_HALO, :]

        @pl.when(i < n_blocks)
        def _():
            ubuf[CONV_HALO:CONV_HALO + tb, :] = ga_ref[k] * jax.nn.sigmoid(gb_ref[k])
            _conv_taps(ubuf, w_ref.at[:, lanes], cb_ref.at[:, lanes], o_ref.at[:, lanes], tb,
                       CONV_HALO - (CONV_W - 1))

        @pl.when(i == n_blocks - 1)
        def _():
            ulast_ref[:, lanes] = ubuf[tb:tb + CONV_HALO, :]

    @pl.when(i >= n_blocks)
    def _():
        o_ref[...] = jnp.zeros(o_ref.shape, o_ref.dtype)


def _conv_prompt(proj3, t, slab_a, slab_b, n_slabs, conv_w, conv_b):
    n_total = proj3.shape[1]
    tb = _pick(t, (512, 256, 128))
    c = n_slabs * LANES
    n_blocks = t // tb
    sg = CONV_SLABS if all(v % CONV_SLABS == 0 for v in (slab_a, slab_b, n_slabs)) else 1
    blk = lambda i: jnp.minimum(i, n_blocks - 1)
    return pl.pallas_call(
        functools.partial(_conv_prompt_kernel, n_blocks=n_blocks),
        grid=(n_slabs // sg, pl.cdiv(n_total, tb)),
        in_specs=[pl.BlockSpec((sg, tb, LANES), lambda s, i: (slab_a // sg + s, blk(i), 0)),
                  pl.BlockSpec((sg, tb, LANES), lambda s, i: (slab_b // sg + s, blk(i), 0)),
                  pl.BlockSpec((CONV_W, sg * LANES), lambda s, i: (0, s)),
                  pl.BlockSpec((1, sg * LANES), lambda s, i: (0, s))],
        out_specs=[pl.BlockSpec((tb, sg * LANES), lambda s, i: (i, s)),
                   pl.BlockSpec((CONV_HALO, sg * LANES), lambda s, i: (0, s))],
        out_shape=[jax.ShapeDtypeStruct((n_total, c), F32), jax.ShapeDtypeStruct((CONV_HALO, c), F32)],
        scratch_shapes=[pltpu.VMEM((sg, CONV_HALO + tb, LANES), F32)],
        compiler_params=_params("parallel", "arbitrary"),
        name="conv_prompt",
    )(proj3, proj3, conv_w, conv_b.reshape(1, c))


SAMPLE_U_PAD = 8


def _conv_sample_kernel(st_ref, ga_ref, gb_ref, w_ref, cb_ref, o_ref, nst_ref, ubuf_ref, *, t_new):
    bsz, hist, _ = st_ref.shape

    def one_sequence(b, carry):
        ubuf_ref[0:hist, :] = st_ref[b]
        ubuf_ref[hist:hist + SAMPLE_U_PAD, :] = ga_ref[b] * jax.nn.sigmoid(gb_ref[b])
        _conv_taps(ubuf_ref, w_ref, cb_ref, o_ref.at[b], SAMPLE_U_PAD, 0)
        nst_ref[b] = ubuf_ref[pl.ds(t_new, hist), :]
        return carry

    lax.fori_loop(0, bsz, one_sequence, 0)


def _conv_sample(state, ga, gb, conv_w, conv_b, t_new):
    bsz, hist, c = state.shape
    assert hist == CONV_W - 1 and t_new <= SAMPLE_U_PAD
    spec = lambda rows: pl.BlockSpec((bsz, rows, LANES), lambda s: (0, 0, s))
    return pl.pallas_call(
        functools.partial(_conv_sample_kernel, t_new=t_new),
        grid=(c // LANES,),
        in_specs=[spec(hist), spec(SAMPLE_U_PAD), spec(SAMPLE_U_PAD),
                  pl.BlockSpec((CONV_W, LANES), lambda s: (0, s)),
                  pl.BlockSpec((1, LANES), lambda s: (0, s))],
        out_specs=[spec(SAMPLE_U_PAD), spec(hist)],
        out_shape=[jax.ShapeDtypeStruct((bsz, SAMPLE_U_PAD, c), F32),
                   jax.ShapeDtypeStruct((bsz, hist, c), F32)],
        scratch_shapes=[pltpu.VMEM((hist + SAMPLE_U_PAD + 2, LANES), F32)],
        compiler_params=_params("parallel"),
        name="conv_sample",
    )(state, ga, gb, conv_w, conv_b.reshape(1, c))


def _mem_attn_kernel(q_ref, k_ref, v_ref, o_ref, *, hd):
    nt = (((1,), (1,)), ((), ()))
    for h in range(q_ref.shape[-1] // hd):
        cols = slice(h * hd, (h + 1) * hd)
        s = lax.dot_general(q_ref[:, cols].astype(BF16), k_ref[:, cols].astype(BF16), nt,
                            preferred_element_type=F32) * (1.0 / math.sqrt(hd))
        m = jnp.max(s, axis=1, keepdims=True)
        p = jnp.exp(s - m)
        l = jnp.sum(p, axis=1, keepdims=True)
        o = jnp.dot(p.astype(BF16), v_ref[:, cols].astype(BF16), preferred_element_type=F32)
        o_ref[:, cols] = (o / l).astype(o_ref.dtype)


def _mem_attn(q, k, v, heads):
    bsz, tq_all, d = q.shape
    m = k.shape[1]
    hd = d // heads
    tq = 1024 if tq_all >= 1024 else tq_all
    hps = heads if tq <= 64 else 1
    qspec = pl.BlockSpec((None, tq, hps * hd), lambda b, h, i: (b, i, h))
    kspec = pl.BlockSpec((None, m, hps * hd), lambda b, h, i: (b, 0, h))
    return pl.pallas_call(
        functools.partial(_mem_attn_kernel, hd=hd),
        grid=(bsz, heads // hps, pl.cdiv(tq_all, tq)),
        in_specs=[qspec, kspec, kspec],
        out_specs=qspec,
        out_shape=jax.ShapeDtypeStruct((bsz, tq_all, d), BF16),
        compiler_params=_params("parallel", "parallel", "arbitrary"),
        name="mem_attn",
    )(q, k, v)


def _router_kernel(x_ref, w_ref, b_ref, eid_ref, gate_ref, *, n_groups, epg):
    logits = jnp.dot(x_ref[...].astype(BF16), w_ref[...].astype(BF16), preferred_element_type=F32) + b_ref[...]
    lane = lax.broadcasted_iota(jnp.int32, logits.shape, 1).astype(F32)
    neg = -jnp.inf
    none = float(LANES)

    def first_max(vals):
        top = jnp.max(vals, axis=1, keepdims=True)
        idx = jnp.min(jnp.where(vals == top, lane, none), axis=1, keepdims=True)
        return top, idx

    g_logits = jnp.where(lane < n_groups, logits, neg)
    g_top, grp = first_max(g_logits)
    p_grp = 1.0 / jnp.sum(jnp.exp(g_logits - g_top), axis=1, keepdims=True)
    lo = n_groups + grp * epg
    e_logits = jnp.where(jnp.logical_and(lane >= lo, lane < lo + epg), logits, neg)
    v1, i1 = first_max(e_logits)
    v2, i2 = first_max(jnp.where(lane == i1, neg, e_logits))
    e21 = jnp.exp(v2 - v1)
    g1 = p_grp * (1.0 / (1.0 + e21))
    g2 = p_grp * (e21 / (1.0 + e21))
    eid = jnp.where(lane == 0.0, i1 - n_groups, jnp.where(lane == 1.0, i2 - n_groups, 0.0))
    eid_ref[...] = eid.astype(jnp.int32)
    gate_ref[...] = jnp.where(lane == 0.0, g1, jnp.where(lane == 1.0, g2, 0.0))


def _router(x, w_rg, b_rg, w_re, b_re):
    n, d = x.shape
    n_groups, epg = w_re.shape[1], w_re.shape[2]
    n_log = n_groups + n_groups * epg
    assert n_log <= LANES
    w = jnp.concatenate([w_rg, w_re.reshape(d, n_groups * epg), jnp.zeros((d, LANES - n_log), F32)], axis=1)
    b = jnp.concatenate([b_rg, b_re.reshape(-1), jnp.zeros((LANES - n_log,), F32)]).reshape(1, LANES)
    tm = _pick(n, (640, 512, 256, 128))
    row = pl.BlockSpec((tm, LANES), lambda i: (i, 0))
    eid, gate = pl.pallas_call(
        functools.partial(_router_kernel, n_groups=n_groups, epg=epg),
        grid=(n // tm,),
        in_specs=[pl.BlockSpec((tm, d), lambda i: (i, 0)),
                  pl.BlockSpec((d, LANES), lambda i: (0, 0)),
                  pl.BlockSpec((1, LANES), lambda i: (0, 0))],
        out_specs=[row, row],
        out_shape=[jax.ShapeDtypeStruct((n, LANES), jnp.int32), jax.ShapeDtypeStruct((n, LANES), F32)],
        compiler_params=_params("parallel"),
        name="router",
    )(x, w, b)
    return eid[:, :2], gate


def _gather_rows_kernel(tok_ref, cnt_ref, x_hbm, o_ref, sem):
    b = pl.program_id(0)
    base = b * EXPERT_BLOCK
    cnt = cnt_ref[b]

    def row_copy(r, src_row):
        return pltpu.make_async_copy(x_hbm.at[pl.ds(src_row, 1), :], o_ref.at[pl.ds(r, 1), :], sem)

    def start_pair(i, carry):
        row_copy(2 * i, tok_ref[base + 2 * i]).start(priority=0)
        row_copy(2 * i + 1, tok_ref[base + 2 * i + 1]).start(priority=1)
        return carry

    def wait(r, carry):
        row_copy(r, 0).wait()
        return carry

    def zero_row(r, carry):
        o_ref[pl.ds(r, 1), :] = jnp.zeros((1, o_ref.shape[1]), o_ref.dtype)
        return carry

    lax.fori_loop(0, jnp.right_shift(cnt, 1), start_pair, 0)

    @pl.when(jnp.bitwise_and(cnt, 1) == 1)
    def _():
        row_copy(cnt - 1, tok_ref[base + cnt - 1]).start()

    lax.fori_loop(cnt, EXPERT_BLOCK, zero_row, 0)
    lax.fori_loop(0, cnt, wait, 0)


def _gather_rows(x, row_tok, blk_cnt):
    n_rows = row_tok.shape[0]
    d = x.shape[1]
    return pl.pallas_call(
        _gather_rows_kernel,
        grid_spec=pltpu.PrefetchScalarGridSpec(
            num_scalar_prefetch=2,
            grid=(n_rows // EXPERT_BLOCK,),
            in_specs=[pl.BlockSpec(memory_space=pl.ANY)],
            out_specs=pl.BlockSpec((EXPERT_BLOCK, d), lambda b, tok, cnt: (b, 0)),
            scratch_shapes=[pltpu.SemaphoreType.DMA(())]),
        out_shape=jax.ShapeDtypeStruct((n_rows, d), x.dtype),
        compiler_params=_params("arbitrary"),
        name="gather_rows",
    )(row_tok, blk_cnt, x)


BLOCK_VALID = 1
BLOCK_NEW_EXPERT = 2
WEIGHT_DMA_PRIORITY = 1
WEIGHT_DMA_SPLIT = 4


def _stream_expert_weights(plan, w_hbms, wbuf_ref, wbf_ref, sem, tile):
    be_ref, fl_ref, nx_ref, seg_ref, nseg_ref = plan
    p = pl.program_id(0)
    b = pl.program_id(1)
    n_seg = nseg_ref[0]
    g = p * n_seg + seg_ref[b]
    slot = jnp.bitwise_and(g, 1)

    def tile_copies(expert, col_pass, dst_slot):
        cols = pl.ds(pl.multiple_of(col_pass * tile, tile), tile)
        copies = []
        for k, w in enumerate(w_hbms):
            chunk = w.shape[1] // WEIGHT_DMA_SPLIT
            for c in range(WEIGHT_DMA_SPLIT):
                rows = pl.ds(c * chunk, chunk)
                copies.append(pltpu.make_async_copy(w.at[expert, rows, cols], wbuf_ref.at[dst_slot, k, rows],
                                                    sem.at[dst_slot, k]))
        return copies

    @pl.when(fl_ref[b] >= BLOCK_NEW_EXPERT)
    def _():
        @pl.when(g == 0)
        def _():
            for cp in tile_copies(be_ref[b], p, slot):
                cp.start(priority=WEIGHT_DMA_PRIORITY)

        @pl.when(g + 1 < pl.num_programs(0) * n_seg)
        def _():
            next_pass = jnp.where(seg_ref[b] + 1 == n_seg, p + 1, p)
            for cp in tile_copies(nx_ref[b], next_pass, 1 - slot):
                cp.start(priority=WEIGHT_DMA_PRIORITY)

        for cp in tile_copies(be_ref[b], p, slot):
            cp.wait()

        for k in range(len(w_hbms)):
            wbf_ref[k] = wbuf_ref[slot, k].astype(BF16)


def _expert_up_kernel(be_ref, fl_ref, nx_ref, seg_ref, nseg_ref, x_ref, wg_hbm, wu_hbm, o_ref, wbuf_ref, wbf_ref, sem):
    _stream_expert_weights((be_ref, fl_ref, nx_ref, seg_ref, nseg_ref), (wg_hbm, wu_hbm), wbuf_ref, wbf_ref, sem,
                           o_ref.shape[1])
    valid = jnp.bitwise_and(fl_ref[pl.program_id(1)], BLOCK_VALID) != 0

    @pl.when(valid)
    def _():
        x_hi, x_lo = [v.astype(BF16) for v in _unpack_bf16_pairs(x_ref[...])]
        half = x_hi.shape[1]
        dot = functools.partial(jnp.dot, preferred_element_type=F32)
        g = dot(x_hi, wbf_ref[0, 0:half, :]) + dot(x_lo, wbf_ref[0, half:2 * half, :])
        u = dot(x_hi, wbf_ref[1, 0:half, :]) + dot(x_lo, wbf_ref[1, half:2 * half, :])
        o_ref[...] = (_silu(g) * u).astype(o_ref.dtype)

    @pl.when(jnp.logical_not(valid))
    def _():
        o_ref[...] = jnp.zeros(o_ref.shape, o_ref.dtype)


def _expert_up(xs, plan, w_gate, w_up):
    n_rows = xs.shape[0]
    d, ff = w_gate.shape[1], w_gate.shape[2]
    tf = _pick(ff, (512, 256, 128))
    return pl.pallas_call(
        _expert_up_kernel,
        grid_spec=pltpu.PrefetchScalarGridSpec(
            num_scalar_prefetch=5,
            grid=(ff // tf, n_rows // EXPERT_BLOCK),
            in_specs=[pl.BlockSpec((EXPERT_BLOCK, d // 2), lambda f, b, *_: (b, 0)),
                      pl.BlockSpec(memory_space=pl.ANY), pl.BlockSpec(memory_space=pl.ANY)],
            out_specs=pl.BlockSpec((EXPERT_BLOCK, tf), lambda f, b, *_: (b, f)),
            scratch_shapes=[pltpu.VMEM((2, 2, d, tf), F32), pltpu.VMEM((2, d, tf), BF16),
                            pltpu.SemaphoreType.DMA((2, 2))]),
        out_shape=jax.ShapeDtypeStruct((n_rows, ff), BF16),
        compiler_params=_params("arbitrary", "arbitrary"),
        name="expert_up",
    )(*plan, xs, w_gate, w_up)


def _down_tile(d):
    return _pick(d, (2048, 1024, 512, 256))


def _expert_down_kernel(be_ref, fl_ref, nx_ref, seg_ref, nseg_ref, h_ref, wd_hbm, o_ref, wbuf_ref, wbf_ref, sem):
    tn = wbf_ref.shape[2]
    _stream_expert_weights((be_ref, fl_ref, nx_ref, seg_ref, nseg_ref), (wd_hbm,), wbuf_ref, wbf_ref, sem, tn)
    valid = jnp.bitwise_and(fl_ref[pl.program_id(1)], BLOCK_VALID) != 0

    @pl.when(valid)
    def _():
        y = jnp.dot(h_ref[...], wbf_ref[0], preferred_element_type=F32)
        o_ref[...] = _pack_bf16_pairs(y[:, :tn // 2], y[:, tn // 2:])

    @pl.when(jnp.logical_not(valid))
    def _():
        o_ref[...] = jnp.zeros(o_ref.shape, o_ref.dtype)


def _expert_down(hb, plan, w_down):
    n_rows, ff = hb.shape
    d = w_down.shape[2]
    tn = _down_tile(d)
    return pl.pallas_call(
        _expert_down_kernel,
        grid_spec=pltpu.PrefetchScalarGridSpec(
            num_scalar_prefetch=5,
            grid=(d // tn, n_rows // EXPERT_BLOCK),
            in_specs=[pl.BlockSpec((EXPERT_BLOCK, ff), lambda n, b, *_: (b, 0)),
                      pl.BlockSpec(memory_space=pl.ANY)],
            out_specs=pl.BlockSpec((EXPERT_BLOCK, tn // 2), lambda n, b, *_: (b, n)),
            scratch_shapes=[pltpu.VMEM((2, 1, ff, tn), F32), pltpu.VMEM((1, ff, tn), BF16),
                            pltpu.SemaphoreType.DMA((2, 1))]),
        out_shape=jax.ShapeDtypeStruct((n_rows, d // 2), jnp.uint32),
        compiler_params=_params("arbitrary", "arbitrary"),
        name="expert_down",
    )(*plan, hb, w_down)


def _combine_ln_kernel(d0_ref, d1_ref, ys_hbm, h_ref, gate_ref, g_ref, b_ref, op_ref, os_ref, buf_ref, sem,
                       *, alpha, prompt_blocks, pack_tile):
    tm = h_ref.shape[0]
    i = pl.program_id(0)
    base = i * tm

    def row_copy(k, r, src_row):
        return pltpu.make_async_copy(ys_hbm.at[pl.ds(src_row, 1), :], buf_ref.at[k, pl.ds(r, 1), :], sem)

    def start(r, carry):
        row_copy(0, r, d0_ref[base + r]).start(priority=0)
        row_copy(1, r, d1_ref[base + r]).start(priority=1)
        return carry

    def wait(r, carry):
        row_copy(0, r, 0).wait()
        row_copy(1, r, 0).wait()
        return carry

    lax.fori_loop(0, tm, start, 0)
    lax.fori_loop(0, tm, wait, 0)
    g0, g1 = gate_ref[:, 0:1], gate_ref[:, 1:2]
    pieces = []
    for n in range(h_ref.shape[1] // pack_tile):
        cols = slice(n * pack_tile // 2, (n + 1) * pack_tile // 2)
        hi0, lo0 = _unpack_bf16_pairs(buf_ref[0, :, cols])
        hi1, lo1 = _unpack_bf16_pairs(buf_ref[1, :, cols])
        pieces += [hi0 * g0 + hi1 * g1, lo0 * g0 + lo1 * g1]
    ffn = jnp.concatenate(pieces, axis=1)
    y = _ln_rows(alpha * h_ref[...] + ffn, g_ref[...], b_ref[...])

    @pl.when(i < prompt_blocks)
    def _():
        op_ref[...] = y

    @pl.when(i >= prompt_blocks)
    def _():
        os_ref[...] = y


def _combine_ln(ys, dest0, dest1, h, gate, g, b, alpha, n_prompt):
    n, d = h.shape
    tm = _pick(math.gcd(n_prompt, n - n_prompt), (128, 64, 8))
    pb = n_prompt // tm
    row = pl.BlockSpec((tm, d), lambda i, d0, d1: (i, 0))
    vec = pl.BlockSpec((1, d), lambda i, d0, d1: (0, 0))
    return pl.pallas_call(
        functools.partial(_combine_ln_kernel, alpha=alpha, prompt_blocks=pb, pack_tile=_down_tile(d)),
        grid_spec=pltpu.PrefetchScalarGridSpec(
            num_scalar_prefetch=2,
            grid=(n // tm,),
            in_specs=[pl.BlockSpec(memory_space=pl.ANY), row,
                      pl.BlockSpec((tm, LANES), lambda i, d0, d1: (i, 0)), vec, vec],
            out_specs=[pl.BlockSpec((tm, d), lambda i, d0, d1: (jnp.minimum(i, pb - 1), 0)),
                       pl.BlockSpec((tm, d), lambda i, d0, d1: (jnp.maximum(i - pb, 0), 0))],
            scratch_shapes=[pltpu.VMEM((2, tm, d // 2), jnp.uint32), pltpu.SemaphoreType.DMA(())]),
        out_shape=[jax.ShapeDtypeStruct((n_prompt, d), F32), jax.ShapeDtypeStruct((n - n_prompt, d), F32)],
        compiler_params=_params("arbitrary"),
        name="combine_ln",
    )(dest0, dest1, ys, h, gate, g.reshape(1, d), b.reshape(1, d))


def _dispatch_plan(eid, n_experts):
    n, k = eid.shape
    flat_e = eid.reshape(-1)
    onehot = (flat_e[:, None] == jnp.arange(n_experts, dtype=jnp.int32)[None, :]).astype(jnp.int32)
    csum = jnp.cumsum(onehot, axis=0)
    rank = jnp.sum(csum * onehot, axis=1) - 1
    counts = csum[-1]
    padded = (counts + EXPERT_BLOCK - 1) // EXPERT_BLOCK * EXPERT_BLOCK
    p_end = jnp.cumsum(padded)
    p_start = p_end - padded
    dest = (p_start[flat_e] + rank).astype(jnp.int32)
    n_blocks = -(-(n * k + n_experts * (EXPERT_BLOCK - 1)) // EXPERT_BLOCK)
    flat_tok = jnp.repeat(jnp.arange(n, dtype=jnp.int32), k)
    row_tok = jnp.zeros((n_blocks * EXPERT_BLOCK,), jnp.int32).at[dest].set(flat_tok, unique_indices=True)
    blk_row = jnp.arange(n_blocks, dtype=jnp.int32) * EXPERT_BLOCK
    blk_e = jnp.minimum(jnp.searchsorted(p_end, blk_row, side='right'), n_experts - 1).astype(jnp.int32)
    valid = blk_row < p_end[-1]
    blk_e = jnp.where(valid, blk_e, blk_e[p_end[-1] // EXPERT_BLOCK - 1])
    blk_cnt = jnp.clip(counts[blk_e] - (blk_row - p_start[blk_e]), 0, EXPERT_BLOCK).astype(jnp.int32)
    new_e = jnp.concatenate([jnp.ones((1,), bool), blk_e[1:] != blk_e[:-1]])
    flags = (valid.astype(jnp.int32) * BLOCK_VALID + new_e.astype(jnp.int32) * BLOCK_NEW_EXPERT)
    seg = jnp.cumsum(new_e.astype(jnp.int32)) - 1
    n_seg = seg[-1:] + 1
    seg_e = jnp.zeros((n_experts,), jnp.int32).at[seg].set(blk_e)
    next_e = seg_e[(seg + 1) % n_seg]
    plan = (blk_e, flags, next_e.astype(jnp.int32), seg.astype(jnp.int32), n_seg.astype(jnp.int32))
    return row_tok, blk_cnt, plan, dest.reshape(n, k)


def _pad_rows(a, rows):
    return jnp.pad(a, ((0, 0), (0, rows - a.shape[1]), (0, 0)))


def _layer(x_p, x_s, mem_prompt, win_k, win_v, conv_state, mem_k_s, mem_v_s,
           w_in, conv_w, conv_b, conv_ln_g, conv_ln_b, w_out, ln1_g, ln1_b,
           w_mem_q, w_mem_k, w_mem_v, w_mem_o, ln2_g, ln2_b,
           w_rg, b_rg, w_re, b_re, w_gate, w_up, w_down, ln3_g, ln3_b, alpha):
    bp, t, d = x_p.shape
    bs, ts, _ = x_s.shape
    past, heads = win_k.shape[1], win_k.shape[2]
    width = heads * HEAD_DIM
    conv_ch = conv_state.shape[2]
    cs = conv_ch // LANES
    mem_tokens, mem_heads = mem_k_s.shape[1], mem_k_s.shape[2]
    n_experts = w_gate.shape[0]
    assert bp == 1 and t % W_MAX == 0 and past == W_MAX and win_k.shape[3] == HEAD_DIM
    assert ts <= SAMPLE_U_PAD and w_in.shape[1] == 3 * width + 2 * conv_ch and width + conv_ch == d
    n_s = bs * ts

    x = jnp.concatenate([x_p.reshape(t, d), x_s.reshape(n_s, d)], axis=0)
    proj3 = _matmul_slabs(x, w_in.astype(BF16))

    def sample_part(lo, hi):
        return jnp.transpose(proj3[lo:hi, t:], (1, 0, 2)).reshape(bs, ts, (hi - lo) * LANES)

    q_s = sample_part(0, heads)
    k_s = sample_part(heads, 2 * heads)
    v_s = sample_part(2 * heads, 3 * heads)
    ga_s = sample_part(3 * heads, 3 * heads + cs)
    gb_s = sample_part(3 * heads + cs, 3 * heads + 2 * cs)

    put_sample = lambda full, rows: lax.dynamic_update_slice(full, rows.astype(full.dtype), (t,) + (0,) * (full.ndim - 1))
    attn_p, attn_s, win_k_s, win_v_s = _attn(proj3, t, heads, _pad_rows(q_s, SAMPLE_Q_PAD),
                                             k_s.reshape(bs, ts, heads, HEAD_DIM),
                                             v_s.reshape(bs, ts, heads, HEAD_DIM), win_k, win_v)
    attn = put_sample(attn_p, attn_s[:, :ts].reshape(n_s, width))
    keep_p = min(W_MAX, t)
    win_k_p = jnp.transpose(proj3[heads:2 * heads, t - keep_p:t], (1, 0, 2))[None]
    win_v_p = jnp.transpose(proj3[2 * heads:3 * heads, t - keep_p:t], (1, 0, 2))[None]

    conv_p, u_last = _conv_prompt(proj3, t, 3 * heads, 3 * heads + cs, cs, conv_w, conv_b)
    conv_state_p = u_last[CONV_HALO - (CONV_W - 1):][None]
    conv_s, conv_state_s = _conv_sample(conv_state, _pad_rows(ga_s, SAMPLE_U_PAD), _pad_rows(gb_s, SAMPLE_U_PAD),
                                        conv_w, conv_b, ts)
    conv_all = put_sample(conv_p, conv_s[:, :ts].reshape(n_s, conv_ch))
    c_all = _ln_silu(conv_all, conv_ln_g, conv_ln_b)

    h1, h1_bf = _res_ln(x, _matmul([attn, c_all], w_out.astype(BF16)), ln1_g, ln1_b, alpha)

    qm = _matmul([h1_bf], w_mem_q.astype(BF16), out_dtype=BF16)
    mem_x = mem_prompt.reshape(bp * mem_tokens, d)
    mem_k_p = _matmul([mem_x], w_mem_k.astype(BF16))
    mem_v_p = _matmul([mem_x], w_mem_v.astype(BF16))
    om_s = _mem_attn(_pad_rows(qm[t:].reshape(bs, ts, d), SAMPLE_Q_PAD), mem_k_s.reshape(bs, mem_tokens, d),
                     mem_v_s.reshape(bs, mem_tokens, d), mem_heads)[:, :ts]
    om = put_sample(_mem_attn(qm[None], mem_k_p[None], mem_v_p[None], mem_heads)[0], om_s.reshape(n_s, d))
    h2, h2_packed = _res_ln(h1, _matmul([om], w_mem_o.astype(BF16)), ln2_g, ln2_b, alpha, pack=True)

    eid, gate = _router(h2, w_rg, b_rg, w_re, b_re)
    row_tok, blk_cnt, plan, dest = _dispatch_plan(eid, n_experts)
    xs = _gather_rows(h2_packed, row_tok, blk_cnt)
    hb = _expert_up(xs, plan, w_gate, w_up)
    ys = _expert_down(hb, plan, w_down)
    y_p, y_s = _combine_ln(ys, dest[:, 0], dest[:, 1], h2, gate, ln3_g, ln3_b, alpha, t)

    mem_shape = (bp, mem_tokens, mem_heads, d // mem_heads)
    return (y_p.reshape(bp, t, d), y_s.reshape(bs, ts, d), win_k_p, win_v_p, conv_state_p,
            mem_k_p.reshape(mem_shape), mem_v_p.reshape(mem_shape), win_k_s, win_v_s, conv_state_s)


def kernel(x_prompt, x_sample, mem_prompt, cache_win_k, cache_win_v, state_conv, cache_mem_k, cache_mem_v, w_in, conv_w, conv_b, conv_ln_g, conv_ln_b, w_out, ln1_g, ln1_b, w_mem_q, w_mem_k, w_mem_v, w_mem_o, ln2_g, ln2_b, w_router_group, b_router_group, w_router_expert, b_router_expert, w_exp_gate, w_exp_up, w_exp_down, ln3_g, ln3_b):
    depth = w_in.shape[0]
    alpha = (2 * depth) ** 0.25
    hp, hs = x_prompt, x_sample
    per_layer = []
    for l in range(depth):
        outs = _layer(hp, hs, mem_prompt, cache_win_k[l], cache_win_v[l], state_conv[l], cache_mem_k[l], cache_mem_v[l],
                      w_in[l], conv_w[l], conv_b[l], conv_ln_g[l], conv_ln_b[l], w_out[l], ln1_g[l], ln1_b[l],
                      w_mem_q[l], w_mem_k[l], w_mem_v[l], w_mem_o[l], ln2_g[l], ln2_b[l],
                      w_router_group[l], b_router_group[l], w_router_expert[l], b_router_expert[l],
                      w_exp_gate[l], w_exp_up[l], w_exp_down[l], ln3_g[l], ln3_b[l], alpha)
        hp, hs = outs[0], outs[1]
        per_layer.append(outs[2:])
    stacked = [jnp.stack([layer[i] for layer in per_layer]) for i in range(8)]
    return (hp, hs, *stacked)
```

```python
import functools
import math

import jax
import jax.numpy as jnp
from jax import lax
from jax.experimental import pallas as pl
from jax.experimental.pallas import tpu as pltpu

F32 = jnp.float32
BF16 = jnp.bfloat16

LANES = 128
HEAD_DIM = 128
DILATED_BRANCHES = ((128, 1), (512, 4), (2048, 16))
W_MAX = 2048
KEYS_PER_BLOCK = 128
ATTN_GROUP = 8
CONV_W = 31
CONV_HALO = 32
EXPERT_BLOCK = 128
LN_EPS = 1e-5
VMEM_LIMIT = 56 * 1024 * 1024


def _params(*sem):
    return pltpu.CompilerParams(dimension_semantics=sem, vmem_limit_bytes=VMEM_LIMIT)


def _pick(n, candidates):
    for c in candidates:
        if n % c == 0:
            return c
    return n


def _mm_kernel(*refs):
    a_refs, b_ref, o_ref = refs[:-2], refs[-2], refs[-1]
    acc = None
    k0 = 0
    for a_ref in a_refs:
        k1 = k0 + a_ref.shape[1]
        part = jnp.dot(a_ref[...].astype(BF16), b_ref[k0:k1, :].astype(BF16), preferred_element_type=F32)
        acc = part if acc is None else acc + part
        k0 = k1
    o_ref[...] = acc.astype(o_ref.dtype)


def _matmul(srcs, b, out_dtype=F32):
    m = srcs[0].shape[0]
    k, n = b.shape
    assert sum(a.shape[1] for a in srcs) == k
    tm = _pick(m, (640, 512, 256, 128))
    tn = _pick(n, (1024, 512, 256, 128))
    return pl.pallas_call(
        _mm_kernel,
        grid=(m // tm, n // tn),
        in_specs=[pl.BlockSpec((tm, a.shape[1]), lambda i, j: (i, 0)) for a in srcs]
        + [pl.BlockSpec((k, tn), lambda i, j: (0, j))],
        out_specs=pl.BlockSpec((tm, tn), lambda i, j: (i, j)),
        out_shape=jax.ShapeDtypeStruct((m, n), out_dtype),
        compiler_params=_params("parallel", "arbitrary"),
        name="matmul",
    )(*srcs, b)


def _mm_slab_kernel(a_ref, b_ref, o_ref):
    r = jnp.dot(a_ref[...].astype(BF16), b_ref[...].astype(BF16), preferred_element_type=F32)
    for s in range(o_ref.shape[0]):
        o_ref[s] = r[:, s * LANES:(s + 1) * LANES]


def _matmul_slabs(a, b):
    m, k = a.shape
    n = b.shape[1]
    tm = _pick(m, (640, 512, 256, 128))
    tn = _pick(n, (1024, 512, 256, 128))
    return pl.pallas_call(
        _mm_slab_kernel,
        grid=(m // tm, n // tn),
        in_specs=[pl.BlockSpec((tm, k), lambda i, j: (i, 0)),
                  pl.BlockSpec((k, tn), lambda i, j: (0, j))],
        out_specs=pl.BlockSpec((tn // LANES, tm, LANES), lambda i, j: (j, i, 0)),
        out_shape=jax.ShapeDtypeStruct((n // LANES, m, LANES), F32),
        compiler_params=_params("parallel", "arbitrary"),
        name="matmul_slabs",
    )(a, b)


def _ln_rows(x, g, b):
    mu = jnp.mean(x, axis=-1, keepdims=True)
    xc = x - mu
    var = jnp.mean(xc * xc, axis=-1, keepdims=True)
    return xc * lax.rsqrt(var + LN_EPS) * g + b


def _silu(x):
    return x * jax.nn.sigmoid(x)


def _pack_bf16_pairs(hi, lo):
    hi = lax.bitcast_convert_type(hi.astype(BF16).astype(F32), jnp.uint32)
    lo = lax.bitcast_convert_type(lo.astype(BF16).astype(F32), jnp.uint32)
    return jnp.bitwise_or(hi, jnp.right_shift(lo, jnp.uint32(16)))


def _unpack_bf16_pairs(u):
    hi = lax.bitcast_convert_type(jnp.bitwise_and(u, jnp.uint32(0xFFFF0000)), F32)
    lo = lax.bitcast_convert_type(jnp.left_shift(u, jnp.uint32(16)), F32)
    return hi, lo


def _res_ln_kernel(res_ref, t_ref, g_ref, b_ref, o_ref, o2_ref, *, alpha, pack):
    y = _ln_rows(alpha * res_ref[...] + t_ref[...], g_ref[...], b_ref[...])
    o_ref[...] = y
    half = y.shape[1] // 2
    o2_ref[...] = _pack_bf16_pairs(y[:, :half], y[:, half:]) if pack else y.astype(BF16)


def _res_ln(res, t, g, b, alpha, pack=False):
    n, d = res.shape
    tm = _pick(n, (128, 64, 8))
    row = pl.BlockSpec((tm, d), lambda i: (i, 0))
    vec = pl.BlockSpec((1, d), lambda i: (0, 0))
    second = (jax.ShapeDtypeStruct((n, d // 2), jnp.uint32), pl.BlockSpec((tm, d // 2), lambda i: (i, 0))) if pack \
        else (jax.ShapeDtypeStruct((n, d), BF16), row)
    return pl.pallas_call(
        functools.partial(_res_ln_kernel, alpha=alpha, pack=pack),
        grid=(n // tm,),
        in_specs=[row, row, vec, vec],
        out_specs=[row, second[1]],
        out_shape=[jax.ShapeDtypeStruct((n, d), F32), second[0]],
        compiler_params=_params("parallel"),
        name="res_ln",
    )(res, t, g.reshape(1, d), b.reshape(1, d))


def _ln_silu_kernel(x_ref, g_ref, b_ref, o_ref):
    o_ref[...] = _silu(_ln_rows(x_ref[...], g_ref[...], b_ref[...])).astype(o_ref.dtype)


def _ln_silu(x, g, b):
    n, d = x.shape
    tm = _pick(n, (256, 128, 64, 8))
    row = pl.BlockSpec((tm, d), lambda i: (i, 0))
    vec = pl.BlockSpec((1, d), lambda i: (0, 0))
    return pl.pallas_call(
        _ln_silu_kernel,
        grid=(n // tm,),
        in_specs=[row, vec, vec],
        out_specs=row,
        out_shape=jax.ShapeDtypeStruct((n, d), BF16),
        compiler_params=_params("parallel"),
        name="ln_silu",
    )(x, g.reshape(1, d), b.reshape(1, d))


def _attn_prompt_body(c, n_chunks, q_ref, kp_ref, kc_ref, vp_ref, vc_ref, o_ref, acc_ref, m_ref, l_ref):
    scale = 1.0 / math.sqrt(HEAD_DIM)
    blk = KEYS_PER_BLOCK
    ii = lax.broadcasted_iota(jnp.int32, (blk, blk), 0)
    jj = lax.broadcasted_iota(jnp.int32, (blk, blk), 1)
    nt = (((1,), (1,)), ((), ()))

    def rows(start, d):
        return pl.ds(start, blk) if d == 1 else pl.ds(start, blk, stride=d)

    def sub_blocks(subs, d, first, last):
        n = range(len(subs))
        cur = [rows(r + d * blk * b, d) for r, b in subs]
        prev = [rows(r + W_MAX - d * blk, d) if b == 0 else rows(r + d * blk * (b - 1), d) for r, b in subs]
        k_prev = [kp_ref if b == 0 else kc_ref for _, b in subs]
        v_prev = [vp_ref if b == 0 else vc_ref for _, b in subs]
        q = [(q_ref[cur[i], :] * scale).astype(BF16) for i in n]
        s_a = [lax.dot_general(q[i], k_prev[i][prev[i], :].astype(BF16), nt, preferred_element_type=F32) for i in n]
        s_b = [lax.dot_general(q[i], kc_ref[cur[i], :].astype(BF16), nt, preferred_element_type=F32) for i in n]
        for i, (_, b) in enumerate(subs):
            mask_a = jj >= ii
            if b == 0:
                mask_a = jnp.logical_and(mask_a, c > 0)
            s_a[i] = jnp.where(mask_a, s_a[i], -jnp.inf)
            s_b[i] = jnp.where(jj <= ii, s_b[i], -jnp.inf)
        m_loc = [jnp.maximum(jnp.max(s_a[i], axis=1, keepdims=True), jnp.max(s_b[i], axis=1, keepdims=True))
                 for i in n]
        p_a = [jnp.exp(s_a[i] - m_loc[i]) for i in n]
        p_b = [jnp.exp(s_b[i] - m_loc[i]) for i in n]
        l_loc = [jnp.sum(p_a[i], axis=1, keepdims=True) + jnp.sum(p_b[i], axis=1, keepdims=True) for i in n]
        acc_loc = [jnp.dot(p_a[i].astype(BF16), v_prev[i][prev[i], :].astype(BF16), preferred_element_type=F32)
                   + jnp.dot(p_b[i].astype(BF16), vc_ref[cur[i], :].astype(BF16), preferred_element_type=F32)
                   for i in n]
        for i in n:
            if first:
                m_new, l_new, acc_new = m_loc[i], l_loc[i], acc_loc[i]
            else:
                m_old = m_ref[cur[i], :][:, :1]
                l_old = l_ref[cur[i], :][:, :1]
                m_new = jnp.maximum(m_old, m_loc[i])
                a_old = jnp.exp(m_old - m_new)
                a_loc = jnp.exp(m_loc[i] - m_new)
                l_new = a_old * l_old + a_loc * l_loc[i]
                acc_new = a_old * acc_ref[cur[i], :] + a_loc * acc_loc[i]
            if last:
                o_ref[cur[i], :] = (acc_new / l_new).astype(o_ref.dtype)
            else:
                m_ref[cur[i], :] = jnp.broadcast_to(m_new, (blk, LANES))
                l_ref[cur[i], :] = jnp.broadcast_to(l_new, (blk, LANES))
                acc_ref[cur[i], :] = acc_new

    @pl.when(c < n_chunks)
    def _():
        order = sorted(DILATED_BRANCHES, key=lambda wd: -wd[1])
        for idx, (w, d) in enumerate(order):
            subs = [(r, b) for b in range(W_MAX // (d * blk)) for r in range(d)]
            for g in range(0, len(subs), ATTN_GROUP):
                sub_blocks(subs[g:g + ATTN_GROUP], d, idx == 0, idx == len(order) - 1)

    @pl.when(c >= n_chunks)
    def _():
        o_ref[...] = jnp.zeros(o_ref.shape, o_ref.dtype)


SAMPLE_Q_PAD = 8
SAMPLE_HEADS = 8


def _key_multiplicity(dist):
    mult = jnp.zeros(dist.shape, F32)
    for (w, d) in DILATED_BRANCHES:
        hit = jnp.logical_and(jnp.bitwise_and(dist, d - 1) == 0, dist <= w)
        mult = mult + jnp.where(hit, 1.0, 0.0)
    return jnp.where(dist >= 0, mult, 0.0)


def _window_copies(b, g, kn_ref, vn_ref, kc_ref, vc_ref, wk_hbm, wv_hbm, sem):
    _, past, hg, _ = kc_ref.shape
    t_new = kn_ref.shape[1]
    heads = pl.ds(pl.multiple_of(g * hg, hg), hg)
    copies = []
    for k, (c_ref, n_ref, w_hbm) in enumerate(((kc_ref, kn_ref, wk_hbm), (vc_ref, vn_ref, wv_hbm))):
        copies.append(pltpu.make_async_copy(c_ref.at[0, pl.ds(t_new, past - t_new)],
                                            w_hbm.at[b, pl.ds(0, past - t_new), heads, :], sem.at[2 * k]))
        copies.append(pltpu.make_async_copy(n_ref.at[0], w_hbm.at[b, pl.ds(past - t_new, t_new), heads, :],
                                            sem.at[2 * k + 1]))
    return copies


def _attn_sample_body(q_ref, kn_ref, vn_ref, kc_ref, vc_ref, o_ref):
    _, past, hg, _ = kc_ref.shape
    t_new = kn_ref.shape[1]
    scale = 1.0 / math.sqrt(HEAD_DIM)
    qi = lax.broadcasted_iota(jnp.int32, (SAMPLE_Q_PAD, past), 0)
    rho = lax.broadcasted_iota(jnp.int32, (SAMPLE_Q_PAD, past), 1)
    mult_c = _key_multiplicity(past + qi - rho)
    qi1 = lax.broadcasted_iota(jnp.int32, (SAMPLE_Q_PAD, 1), 0)
    mult_n = [_key_multiplicity(qi1 - i) for i in range(t_new)]
    nt = (((1,), (1,)), ((), ()))
    k_rows = kc_ref.at[0].reshape(past * hg, HEAD_DIM)
    v_rows = vc_ref.at[0].reshape(past * hg, HEAD_DIM)
    for h in range(hg):
        sl = slice(h * HEAD_DIM, (h + 1) * HEAD_DIM)
        head_rows = pl.ds(h, past, stride=hg)
        q = q_ref[0, :, sl] * scale
        s_c = lax.dot_general(q.astype(BF16), k_rows[head_rows, :].astype(BF16), nt, preferred_element_type=F32)
        s_c = jnp.where(mult_c > 0, s_c, -jnp.inf)
        s_n = [jnp.sum(q * kn_ref[0, i, h:h + 1, :], axis=1, keepdims=True) for i in range(t_new)]
        s_n = [jnp.where(mult_n[i] > 0, s_n[i], -jnp.inf) for i in range(t_new)]
        m = jnp.max(s_c, axis=1, keepdims=True)
        for i in range(t_new):
            m = jnp.maximum(m, s_n[i])
        p_c = mult_c * jnp.exp(s_c - m)
        l = jnp.sum(p_c, axis=1, keepdims=True)
        o = jnp.dot(p_c.astype(BF16), v_rows[head_rows, :].astype(BF16), preferred_element_type=F32)
        for i in range(t_new):
            p_n = mult_n[i] * jnp.exp(s_n[i] - m)
            l = l + p_n
            o = o + p_n * vn_ref[0, i, h:h + 1, :]
        o_ref[0, :, sl] = o / l


def _attn_kernel(*refs, n_chunks, heads, prompt_steps, sample_steps, sample_groups):
    prompt_in, sample_in = refs[0:5], refs[5:10]
    o_ref, so_ref, wk_hbm, wv_hbm, acc_ref, m_ref, l_ref, sem = refs[10:]
    s = pl.program_id(0)
    window_copies = functools.partial(_window_copies, s // sample_groups, s % sample_groups, *sample_in[1:],
                                      wk_hbm, wv_hbm, sem)

    @pl.when(s < sample_steps)
    def _():
        for cp in window_copies():
            cp.start()
        _attn_sample_body(*sample_in, so_ref)

    @pl.when(s < prompt_steps)
    def _():
        _attn_prompt_body(s // heads, n_chunks, *prompt_in, o_ref, acc_ref, m_ref, l_ref)

    @pl.when(s < sample_steps)
    def _():
        for cp in window_copies():
            cp.wait()


def _attn(proj3, t, heads, q, k_new, v_new, k_cache, v_cache):
    n_total = proj3.shape[1]
    n_chunks = t // W_MAX
    bsz, past, _, _ = k_cache.shape
    t_new = k_new.shape[1]
    hg = SAMPLE_HEADS if heads % SAMPLE_HEADS == 0 else heads
    groups = heads // hg
    prompt_steps = pl.cdiv(n_total, W_MAX) * heads
    sample_steps = bsz * groups

    ps = lambda s: jnp.minimum(s, prompt_steps - 1)
    chunk = lambda s: jnp.minimum(ps(s) // heads, n_chunks - 1)
    blk = (None, W_MAX, LANES)
    cur = lambda off: pl.BlockSpec(blk, lambda s: (off + ps(s) % heads, chunk(s), 0))
    prev = lambda off: pl.BlockSpec(blk, lambda s: (off + ps(s) % heads, jnp.maximum(chunk(s) - 1, 0), 0))
    ss = lambda s: jnp.minimum(s, sample_steps - 1)
    qspec = pl.BlockSpec((1, SAMPLE_Q_PAD, hg * HEAD_DIM), lambda s: (ss(s) // groups, 0, ss(s) % groups))
    kv = lambda rows: pl.BlockSpec((1, rows, hg, HEAD_DIM), lambda s: (ss(s) // groups, 0, ss(s) % groups, 0))
    win = jax.ShapeDtypeStruct(k_cache.shape, F32)
    return pl.pallas_call(
        functools.partial(_attn_kernel, n_chunks=n_chunks, heads=heads, prompt_steps=prompt_steps,
                          sample_steps=sample_steps, sample_groups=groups),
        grid=(max(prompt_steps, sample_steps),),
        in_specs=[cur(0), prev(heads), cur(heads), prev(2 * heads), cur(2 * heads),
                  qspec, kv(t_new), kv(t_new), kv(past), kv(past)],
        out_specs=[pl.BlockSpec((W_MAX, LANES), lambda s: (ps(s) // heads, ps(s) % heads)), qspec,
                   pl.BlockSpec(memory_space=pl.ANY), pl.BlockSpec(memory_space=pl.ANY)],
        out_shape=[jax.ShapeDtypeStruct((n_total, heads * LANES), BF16),
                   jax.ShapeDtypeStruct((bsz, SAMPLE_Q_PAD, heads * HEAD_DIM), F32), win, win],
        scratch_shapes=[pltpu.VMEM((W_MAX, LANES), F32)] * 3 + [pltpu.SemaphoreType.DMA((4,))],
        compiler_params=_params("arbitrary"),
        name="attn",
    )(proj3, proj3, proj3, proj3, proj3, q, k_new, v_new, k_cache, v_cache)


CONV_ROWS = 64


def _conv_taps(ubuf_ref, w_ref, cb_ref, out_ref, n_rows, base):
    for r0 in range(0, n_rows, CONV_ROWS):
        nr = min(CONV_ROWS, n_rows - r0)
        acc = jnp.broadcast_to(cb_ref[...], (nr, LANES))
        for j in range(CONV_W):
            acc = acc + ubuf_ref[pl.ds(base + r0 + j, nr), :] * w_ref[j:j + 1, :]
        out_ref[pl.ds(r0, nr), :] = acc


CONV_SLABS = 2


def _conv_prompt_kernel(ga_ref, gb_ref, w_ref, cb_ref, o_ref, ulast_ref, ubuf_ref, *, n_blocks):
    i = pl.program_id(1)
    n_slabs, tb, _ = ga_ref.shape
    for k in range(n_slabs):
        lanes = pl.ds(k * LANES, LANES)
        ubuf = ubuf_ref.at[k]

        @pl.when(i == 0)
        def _():
            ubuf[0:CONV_HALO, :] = jnp.zeros((CONV_HALO, LANES), F32)

        @pl.when(jnp.logical_and(i > 0, i < n_blocks))
        def _():
            ubuf[0:CONV_HALO, :] = ubuf[tb:tb + CONV_HALO, :]

        @pl.when(i < n_blocks)
        def _():
            ubuf[CONV_HALO:CONV_HALO + tb, :] = ga_ref[k] * jax.nn.sigmoid(gb_ref[k])
            _conv_taps(ubuf, w_ref.at[:, lanes], cb_ref.at[:, lanes], o_ref.at[:, lanes], tb,
                       CONV_HALO - (CONV_W - 1))

        @pl.when(i == n_blocks - 1)
        def _():
            ulast_ref[:, lanes] = ubuf[tb:tb + CONV_HALO, :]

    @pl.when(i >= n_blocks)
    def _():
        o_ref[...] = jnp.zeros(o_ref.shape, o_ref.dtype)


def _conv_prompt(proj3, t, slab_a, slab_b, n_slabs, conv_w, conv_b):
    n_total = proj3.shape[1]
    tb = _pick(t, (512, 256, 128))
    c = n_slabs * LANES
    n_blocks = t // tb
    sg = CONV_SLABS if all(v % CONV_SLABS == 0 for v in (slab_a, slab_b, n_slabs)) else 1
    blk = lambda i: jnp.minimum(i, n_blocks - 1)
    return pl.pallas_call(
        functools.partial(_conv_prompt_kernel, n_blocks=n_blocks),
        grid=(n_slabs // sg, pl.cdiv(n_total, tb)),
        in_specs=[pl.BlockSpec((sg, tb, LANES), lambda s, i: (slab_a // sg + s, blk(i), 0)),
                  pl.BlockSpec((sg, tb, LANES), lambda s, i: (slab_b // sg + s, blk(i), 0)),
                  pl.BlockSpec((CONV_W, sg * LANES), lambda s, i: (0, s)),
                  pl.BlockSpec((1, sg * LANES), lambda s, i: (0, s))],
        out_specs=[pl.BlockSpec((tb, sg * LANES), lambda s, i: (i, s)),
                   pl.BlockSpec((CONV_HALO, sg * LANES), lambda s, i: (0, s))],
        out_shape=[jax.ShapeDtypeStruct((n_total, c), F32), jax.ShapeDtypeStruct((CONV_HALO, c), F32)],
        scratch_shapes=[pltpu.VMEM((sg, CONV_HALO + tb, LANES), F32)],
        compiler_params=_params("parallel", "arbitrary"),
        name="conv_prompt",
    )(proj3, proj3, conv_w, conv_b.reshape(1, c))


SAMPLE_U_PAD = 8


def _conv_sample_kernel(st_ref, ga_ref, gb_ref, w_ref, cb_ref, o_ref, nst_ref, ubuf_ref, *, t_new):
    bsz, hist, _ = st_ref.shape

    def one_sequence(b, carry):
        ubuf_ref[0:hist, :] = st_ref[b]
        ubuf_ref[hist:hist + SAMPLE_U_PAD, :] = ga_ref[b] * jax.nn.sigmoid(gb_ref[b])
        _conv_taps(ubuf_ref, w_ref, cb_ref, o_ref.at[b], SAMPLE_U_PAD, 0)
        nst_ref[b] = ubuf_ref[pl.ds(t_new, hist), :]
        return carry

    lax.fori_loop(0, bsz, one_sequence, 0)


def _conv_sample(state, ga, gb, conv_w, conv_b, t_new):
    bsz, hist, c = state.shape
    assert hist == CONV_W - 1 and t_new <= SAMPLE_U_PAD
    spec = lambda rows: pl.BlockSpec((bsz, rows, LANES), lambda s: (0, 0, s))
    return pl.pallas_call(
        functools.partial(_conv_sample_kernel, t_new=t_new),
        grid=(c // LANES,),
        in_specs=[spec(hist), spec(SAMPLE_U_PAD), spec(SAMPLE_U_PAD),
                  pl.BlockSpec((CONV_W, LANES), lambda s: (0, s)),
                  pl.BlockSpec((1, LANES), lambda s: (0, s))],
        out_specs=[spec(SAMPLE_U_PAD), spec(hist)],
        out_shape=[jax.ShapeDtypeStruct((bsz, SAMPLE_U_PAD, c), F32),
                   jax.ShapeDtypeStruct((bsz, hist, c), F32)],
        scratch_shapes=[pltpu.VMEM((hist + SAMPLE_U_PAD + 2, LANES), F32)],
        compiler_params=_params("parallel"),
        name="conv_sample",
    )(state, ga, gb, conv_w, conv_b.reshape(1, c))


def _mem_attn_kernel(q_ref, k_ref, v_ref, o_ref, *, hd):
    nt = (((1,), (1,)), ((), ()))
    for h in range(q_ref.shape[-1] // hd):
        cols = slice(h * hd, (h + 1) * hd)
        s = lax.dot_general(q_ref[:, cols].astype(BF16), k_ref[:, cols].astype(BF16), nt,
                            preferred_element_type=F32) * (1.0 / math.sqrt(hd))
        m = jnp.max(s, axis=1, keepdims=True)
        p = jnp.exp(s - m)
        l = jnp.sum(p, axis=1, keepdims=True)
        o = jnp.dot(p.astype(BF16), v_ref[:, cols].astype(BF16), preferred_element_type=F32)
        o_ref[:, cols] = (o / l).astype(o_ref.dtype)


def _mem_attn(q, k, v, heads):
    bsz, tq_all, d = q.shape
    m = k.shape[1]
    hd = d // heads
    tq = 1024 if tq_all >= 1024 else tq_all
    hps = heads if tq <= 64 else 1
    qspec = pl.BlockSpec((None, tq, hps * hd), lambda b, h, i: (b, i, h))
    kspec = pl.BlockSpec((None, m, hps * hd), lambda b, h, i: (b, 0, h))
    return pl.pallas_call(
        functools.partial(_mem_attn_kernel, hd=hd),
        grid=(bsz, heads // hps, pl.cdiv(tq_all, tq)),
        in_specs=[qspec, kspec, kspec],
        out_specs=qspec,
        out_shape=jax.ShapeDtypeStruct((bsz, tq_all, d), BF16),
        compiler_params=_params("parallel", "parallel", "arbitrary"),
        name="mem_attn",
    )(q, k, v)


def _router_kernel(res_ref, t_ref, g_ref, lb_ref, w_ref, b_ref, o_ref, o2_ref, eid_ref, gate_ref,
                   *, alpha, n_groups, epg):
    x = _ln_rows(alpha * res_ref[...] + t_ref[...], g_ref[...], lb_ref[...])
    o_ref[...] = x
    half = x.shape[1] // 2
    o2_ref[...] = _pack_bf16_pairs(x[:, :half], x[:, half:])
    logits = jnp.dot(x.astype(BF16), w_ref[...], preferred_element_type=F32) + b_ref[...]
    lane = lax.broadcasted_iota(jnp.int32, logits.shape, 1).astype(F32)
    neg = -jnp.inf
    none = float(LANES)

    def first_max(vals):
        top = jnp.max(vals, axis=1, keepdims=True)
        idx = jnp.min(jnp.where(vals == top, lane, none), axis=1, keepdims=True)
        return top, idx

    g_logits = jnp.where(lane < n_groups, logits, neg)
    g_top, grp = first_max(g_logits)
    p_grp = 1.0 / jnp.sum(jnp.exp(g_logits - g_top), axis=1, keepdims=True)
    lo = n_groups + grp * epg
    e_logits = jnp.where(jnp.logical_and(lane >= lo, lane < lo + epg), logits, neg)
    v1, i1 = first_max(e_logits)
    v2, i2 = first_max(jnp.where(lane == i1, neg, e_logits))
    e21 = jnp.exp(v2 - v1)
    g1 = p_grp * (1.0 / (1.0 + e21))
    g2 = p_grp * (e21 / (1.0 + e21))
    eid = jnp.where(lane == 0.0, i1 - n_groups, jnp.where(lane == 1.0, i2 - n_groups, 0.0))
    eid_ref[...] = eid.astype(jnp.int32)
    gate_ref[...] = jnp.where(lane == 0.0, g1, jnp.where(lane == 1.0, g2, 0.0))


def _ln_router(res, t, g, lb, alpha, w_rg, b_rg, w_re, b_re):
    n, d = res.shape
    n_groups, epg = w_re.shape[1], w_re.shape[2]
    n_log = n_groups + n_groups * epg
    assert n_log <= LANES
    w = jnp.concatenate([w_rg, w_re.reshape(d, n_groups * epg), jnp.zeros((d, LANES - n_log), F32)], axis=1)
    b = jnp.concatenate([b_rg, b_re.reshape(-1), jnp.zeros((LANES - n_log,), F32)]).reshape(1, LANES)
    tm = _pick(n, (128, 64, 8))
    row = pl.BlockSpec((tm, d), lambda i: (i, 0))
    vec = pl.BlockSpec((1, d), lambda i: (0, 0))
    lane_row = pl.BlockSpec((tm, LANES), lambda i: (i, 0))
    x, x_packed, eid, gate = pl.pallas_call(
        functools.partial(_router_kernel, alpha=alpha, n_groups=n_groups, epg=epg),
        grid=(n // tm,),
        in_specs=[row, row, vec, vec,
                  pl.BlockSpec((d, LANES), lambda i: (0, 0)),
                  pl.BlockSpec((1, LANES), lambda i: (0, 0))],
        out_specs=[row, pl.BlockSpec((tm, d // 2), lambda i: (i, 0)), lane_row, lane_row],
        out_shape=[jax.ShapeDtypeStruct((n, d), F32), jax.ShapeDtypeStruct((n, d // 2), jnp.uint32),
                   jax.ShapeDtypeStruct((n, LANES), jnp.int32), jax.ShapeDtypeStruct((n, LANES), F32)],
        compiler_params=_params("parallel"),
        name="ln_router",
    )(res, t, g.reshape(1, d), lb.reshape(1, d), w.astype(BF16), b)
    return x, x_packed, eid[:, :2], gate


def _gather_rows_kernel(tok_ref, cnt_ref, x_hbm, o_ref, sem):
    b = pl.program_id(0)
    base = b * EXPERT_BLOCK
    cnt = cnt_ref[b]

    def row_copy(r, src_row):
        return pltpu.make_async_copy(x_hbm.at[pl.ds(src_row, 1), :], o_ref.at[pl.ds(r, 1), :], sem)

    def start_pair(i, carry):
        row_copy(2 * i, tok_ref[base + 2 * i]).start(priority=0)
        row_copy(2 * i + 1, tok_ref[base + 2 * i + 1]).start(priority=1)
        return carry

    def wait(r, carry):
        row_copy(r, 0).wait()
        return carry

    def zero_row(r, carry):
        o_ref[pl.ds(r, 1), :] = jnp.zeros((1, o_ref.shape[1]), o_ref.dtype)
        return carry

    lax.fori_loop(0, jnp.right_shift(cnt, 1), start_pair, 0)

    @pl.when(jnp.bitwise_and(cnt, 1) == 1)
    def _():
        row_copy(cnt - 1, tok_ref[base + cnt - 1]).start()

    lax.fori_loop(cnt, EXPERT_BLOCK, zero_row, 0)
    lax.fori_loop(0, cnt, wait, 0)


def _gather_rows(x, row_tok, blk_cnt):
    n_rows = row_tok.shape[0]
    d = x.shape[1]
    return pl.pallas_call(
        _gather_rows_kernel,
        grid_spec=pltpu.PrefetchScalarGridSpec(
            num_scalar_prefetch=2,
            grid=(n_rows // EXPERT_BLOCK,),
            in_specs=[pl.BlockSpec(memory_space=pl.ANY)],
            out_specs=pl.BlockSpec((EXPERT_BLOCK, d), lambda b, tok, cnt: (b, 0)),
            scratch_shapes=[pltpu.SemaphoreType.DMA(())]),
        out_shape=jax.ShapeDtypeStruct((n_rows, d), x.dtype),
        compiler_params=_params("arbitrary"),
        name="gather_rows",
    )(row_tok, blk_cnt, x)


BLOCK_VALID = 1
BLOCK_NEW_EXPERT = 2
WEIGHT_DMA_PRIORITY = 1
WEIGHT_DMA_SPLIT = 4


def _stream_expert_weights(plan, w_hbms, wbuf_ref, wbf_ref, sem, tile):
    be_ref, fl_ref, nx_ref, seg_ref, nseg_ref = plan
    p = pl.program_id(0)
    b = pl.program_id(1)
    n_seg = nseg_ref[0]
    g = p * n_seg + seg_ref[b]
    slot = jnp.bitwise_and(g, 1)

    def tile_copies(expert, col_pass, dst_slot):
        cols = pl.ds(pl.multiple_of(col_pass * tile, tile), tile)
        copies = []
        for k, w in enumerate(w_hbms):
            chunk = w.shape[1] // WEIGHT_DMA_SPLIT
            for c in range(WEIGHT_DMA_SPLIT):
                rows = pl.ds(c * chunk, chunk)
                copies.append(pltpu.make_async_copy(w.at[expert, rows, cols], wbuf_ref.at[dst_slot, k, rows],
                                                    sem.at[dst_slot, k]))
        return copies

    @pl.when(fl_ref[b] >= BLOCK_NEW_EXPERT)
    def _():
        @pl.when(g == 0)
        def _():
            for cp in tile_copies(be_ref[b], p, slot):
                cp.start(priority=WEIGHT_DMA_PRIORITY)

        @pl.when(g + 1 < pl.num_programs(0) * n_seg)
        def _():
            next_pass = jnp.where(seg_ref[b] + 1 == n_seg, p + 1, p)
            for cp in tile_copies(nx_ref[b], next_pass, 1 - slot):
                cp.start(priority=WEIGHT_DMA_PRIORITY)

        for cp in tile_copies(be_ref[b], p, slot):
            cp.wait()

        for k in range(len(w_hbms)):
            wbf_ref[k] = wbuf_ref[slot, k].astype(BF16)


def _expert_up_kernel(be_ref, fl_ref, nx_ref, seg_ref, nseg_ref, x_ref, wg_hbm, wu_hbm, o_ref, wbuf_ref, wbf_ref, sem):
    _stream_expert_weights((be_ref, fl_ref, nx_ref, seg_ref, nseg_ref), (wg_hbm, wu_hbm), wbuf_ref, wbf_ref, sem,
                           o_ref.shape[1])
    valid = jnp.bitwise_and(fl_ref[pl.program_id(1)], BLOCK_VALID) != 0

    @pl.when(valid)
    def _():
        x_hi, x_lo = [v.astype(BF16) for v in _unpack_bf16_pairs(x_ref[...])]
        half = x_hi.shape[1]
        dot = functools.partial(jnp.dot, preferred_element_type=F32)
        g = dot(x_hi, wbf_ref[0, 0:half, :]) + dot(x_lo, wbf_ref[0, half:2 * half, :])
        u = dot(x_hi, wbf_ref[1, 0:half, :]) + dot(x_lo, wbf_ref[1, half:2 * half, :])
        o_ref[...] = (_silu(g) * u).astype(o_ref.dtype)

    @pl.when(jnp.logical_not(valid))
    def _():
        o_ref[...] = jnp.zeros(o_ref.shape, o_ref.dtype)


def _expert_up(xs, plan, w_gate, w_up):
    n_rows = xs.shape[0]
    d, ff = w_gate.shape[1], w_gate.shape[2]
    tf = _pick(ff, (512, 256, 128))
    return pl.pallas_call(
        _expert_up_kernel,
        grid_spec=pltpu.PrefetchScalarGridSpec(
            num_scalar_prefetch=5,
            grid=(ff // tf, n_rows // EXPERT_BLOCK),
            in_specs=[pl.BlockSpec((EXPERT_BLOCK, d // 2), lambda f, b, *_: (b, 0)),
                      pl.BlockSpec(memory_space=pl.ANY), pl.BlockSpec(memory_space=pl.ANY)],
            out_specs=pl.BlockSpec((EXPERT_BLOCK, tf), lambda f, b, *_: (b, f)),
            scratch_shapes=[pltpu.VMEM((2, 2, d, tf), F32), pltpu.VMEM((2, d, tf), BF16),
                            pltpu.SemaphoreType.DMA((2, 2))]),
        out_shape=jax.ShapeDtypeStruct((n_rows, ff), BF16),
        compiler_params=_params("arbitrary", "arbitrary"),
        name="expert_up",
    )(*plan, xs, w_gate, w_up)


def _down_tile(d):
    return _pick(d, (2048, 1024, 512, 256))


def _expert_down_kernel(be_ref, fl_ref, nx_ref, seg_ref, nseg_ref, h_ref, wd_hbm, o_ref, wbuf_ref, wbf_ref, sem):
    tn = wbf_ref.shape[2]
    _stream_expert_weights((be_ref, fl_ref, nx_ref, seg_ref, nseg_ref), (wd_hbm,), wbuf_ref, wbf_ref, sem, tn)
    valid = jnp.bitwise_and(fl_ref[pl.program_id(1)], BLOCK_VALID) != 0

    @pl.when(valid)
    def _():
        y = jnp.dot(h_ref[...], wbf_ref[0], preferred_element_type=F32)
        o_ref[...] = _pack_bf16_pairs(y[:, :tn // 2], y[:, tn // 2:])

    @pl.when(jnp.logical_not(valid))
    def _():
        o_ref[...] = jnp.zeros(o_ref.shape, o_ref.dtype)


def _expert_down(hb, plan, w_down):
    n_rows, ff = hb.shape
    d = w_down.shape[2]
    tn = _down_tile(d)
    return pl.pallas_call(
        _expert_down_kernel,
        grid_spec=pltpu.PrefetchScalarGridSpec(
            num_scalar_prefetch=5,
            grid=(d // tn, n_rows // EXPERT_BLOCK),
            in_specs=[pl.BlockSpec((EXPERT_BLOCK, ff), lambda n, b, *_: (b, 0)),
                      pl.BlockSpec(memory_space=pl.ANY)],
            out_specs=pl.BlockSpec((EXPERT_BLOCK, tn // 2), lambda n, b, *_: (b, n)),
            scratch_shapes=[pltpu.VMEM((2, 1, ff, tn), F32), pltpu.VMEM((1, ff, tn), BF16),
                            pltpu.SemaphoreType.DMA((2, 1))]),
        out_shape=jax.ShapeDtypeStruct((n_rows, d // 2), jnp.uint32),
        compiler_params=_params("arbitrary", "arbitrary"),
        name="expert_down",
    )(*plan, hb, w_down)


def _combine_ln_kernel(d0_ref, d1_ref, ys_hbm, h_ref, gate_ref, g_ref, b_ref, op_ref, os_ref, buf_ref, sem,
                       *, alpha, prompt_blocks, pack_tile):
    tm = h_ref.shape[0]
    i = pl.program_id(0)
    base = i * tm

    def row_copy(k, r, src_row):
        return pltpu.make_async_copy(ys_hbm.at[pl.ds(src_row, 1), :], buf_ref.at[k, pl.ds(r, 1), :], sem)

    def start(r, carry):
        row_copy(0, r, d0_ref[base + r]).start(priority=0)
        row_copy(1, r, d1_ref[base + r]).start(priority=1)
        return carry

    def wait(r, carry):
        row_copy(0, r, 0).wait()
        row_copy(1, r, 0).wait()
        return carry

    lax.fori_loop(0, tm, start, 0)
    lax.fori_loop(0, tm, wait, 0)
    g0, g1 = gate_ref[:, 0:1], gate_ref[:, 1:2]
    pieces = []
    for n in range(h_ref.shape[1] // pack_tile):
        cols = slice(n * pack_tile // 2, (n + 1) * pack_tile // 2)
        hi0, lo0 = _unpack_bf16_pairs(buf_ref[0, :, cols])
        hi1, lo1 = _unpack_bf16_pairs(buf_ref[1, :, cols])
        pieces += [hi0 * g0 + hi1 * g1, lo0 * g0 + lo1 * g1]
    ffn = jnp.concatenate(pieces, axis=1)
    y = _ln_rows(alpha * h_ref[...] + ffn, g_ref[...], b_ref[...])

    @pl.when(i < prompt_blocks)
    def _():
        op_ref[...] = y

    @pl.when(i >= prompt_blocks)
    def _():
        os_ref[...] = y


def _combine_ln(ys, dest0, dest1, h, gate, g, b, alpha, n_prompt):
    n, d = h.shape
    tm = _pick(math.gcd(n_prompt, n - n_prompt), (128, 64, 8))
    pb = n_prompt // tm
    row = pl.BlockSpec((tm, d), lambda i, d0, d1: (i, 0))
    vec = pl.BlockSpec((1, d), lambda i, d0, d1: (0, 0))
    return pl.pallas_call(
        functools.partial(_combine_ln_kernel, alpha=alpha, prompt_blocks=pb, pack_tile=_down_tile(d)),
        grid_spec=pltpu.PrefetchScalarGridSpec(
            num_scalar_prefetch=2,
            grid=(n // tm,),
            in_specs=[pl.BlockSpec(memory_space=pl.ANY), row,
                      pl.BlockSpec((tm, LANES), lambda i, d0, d1: (i, 0)), vec, vec],
            out_specs=[pl.BlockSpec((tm, d), lambda i, d0, d1: (jnp.minimum(i, pb - 1), 0)),
                       pl.BlockSpec((tm, d), lambda i, d0, d1: (jnp.maximum(i - pb, 0), 0))],
            scratch_shapes=[pltpu.VMEM((2, tm, d // 2), jnp.uint32), pltpu.SemaphoreType.DMA(())]),
        out_shape=[jax.ShapeDtypeStruct((n_prompt, d), F32), jax.ShapeDtypeStruct((n - n_prompt, d), F32)],
        compiler_params=_params("arbitrary"),
        name="combine_ln",
    )(dest0, dest1, ys, h, gate, g.reshape(1, d), b.reshape(1, d))


def _dispatch_plan(eid, n_experts):
    n, k = eid.shape
    flat_e = eid.reshape(-1)
    onehot = (flat_e[:, None] == jnp.arange(n_experts, dtype=jnp.int32)[None, :]).astype(jnp.int32)
    csum = jnp.cumsum(onehot, axis=0)
    rank = jnp.sum(csum * onehot, axis=1) - 1
    counts = csum[-1]
    padded = (counts + EXPERT_BLOCK - 1) // EXPERT_BLOCK * EXPERT_BLOCK
    p_end = jnp.cumsum(padded)
    p_start = p_end - padded
    dest = (p_start[flat_e] + rank).astype(jnp.int32)
    n_blocks = -(-(n * k + n_experts * (EXPERT_BLOCK - 1)) // EXPERT_BLOCK)
    flat_tok = jnp.repeat(jnp.arange(n, dtype=jnp.int32), k)
    row_tok = jnp.zeros((n_blocks * EXPERT_BLOCK,), jnp.int32).at[dest].set(flat_tok, unique_indices=True)
    blk_row = jnp.arange(n_blocks, dtype=jnp.int32) * EXPERT_BLOCK
    blk_e = jnp.minimum(jnp.searchsorted(p_end, blk_row, side='right'), n_experts - 1).astype(jnp.int32)
    valid = blk_row < p_end[-1]
    blk_e = jnp.where(valid, blk_e, blk_e[p_end[-1] // EXPERT_BLOCK - 1])
    blk_cnt = jnp.clip(counts[blk_e] - (blk_row - p_start[blk_e]), 0, EXPERT_BLOCK).astype(jnp.int32)
    new_e = jnp.concatenate([jnp.ones((1,), bool), blk_e[1:] != blk_e[:-1]])
    flags = (valid.astype(jnp.int32) * BLOCK_VALID + new_e.astype(jnp.int32) * BLOCK_NEW_EXPERT)
    seg = jnp.cumsum(new_e.astype(jnp.int32)) - 1
    n_seg = seg[-1:] + 1
    seg_e = jnp.zeros((n_experts,), jnp.int32).at[seg].set(blk_e)
    next_e = seg_e[(seg + 1) % n_seg]
    plan = (blk_e, flags, next_e.astype(jnp.int32), seg.astype(jnp.int32), n_seg.astype(jnp.int32))
    return row_tok, blk_cnt, plan, dest.reshape(n, k)


def _pad_rows(a, rows):
    return jnp.pad(a, ((0, 0), (0, rows - a.shape[1]), (0, 0)))


def _layer(x_p, x_s, mem_prompt, win_k, win_v, conv_state, mem_k_s, mem_v_s,
           w_in, conv_w, conv_b, conv_ln_g, conv_ln_b, w_out, ln1_g, ln1_b,
           w_mem_q, w_mem_k, w_mem_v, w_mem_o, ln2_g, ln2_b,
           w_rg, b_rg, w_re, b_re, w_gate, w_up, w_down, ln3_g, ln3_b, alpha):
    bp, t, d = x_p.shape
    bs, ts, _ = x_s.shape
    past, heads = win_k.shape[1], win_k.shape[2]
    width = heads * HEAD_DIM
    conv_ch = conv_state.shape[2]
    cs = conv_ch // LANES
    mem_tokens, mem_heads = mem_k_s.shape[1], mem_k_s.shape[2]
    n_experts = w_gate.shape[0]
    assert bp == 1 and t % W_MAX == 0 and past == W_MAX and win_k.shape[3] == HEAD_DIM
    assert ts <= SAMPLE_U_PAD and w_in.shape[1] == 3 * width + 2 * conv_ch and width + conv_ch == d
    n_s = bs * ts

    x = jnp.concatenate([x_p.reshape(t, d), x_s.reshape(n_s, d)], axis=0)
    proj3 = _matmul_slabs(x, w_in.astype(BF16))

    def sample_part(lo, hi):
        return jnp.transpose(proj3[lo:hi, t:], (1, 0, 2)).reshape(bs, ts, (hi - lo) * LANES)

    q_s = sample_part(0, heads)
    k_s = sample_part(heads, 2 * heads)
    v_s = sample_part(2 * heads, 3 * heads)
    ga_s = sample_part(3 * heads, 3 * heads + cs)
    gb_s = sample_part(3 * heads + cs, 3 * heads + 2 * cs)

    put_sample = lambda full, rows: lax.dynamic_update_slice(full, rows.astype(full.dtype), (t,) + (0,) * (full.ndim - 1))
    attn_p, attn_s, win_k_s, win_v_s = _attn(proj3, t, heads, _pad_rows(q_s, SAMPLE_Q_PAD),
                                             k_s.reshape(bs, ts, heads, HEAD_DIM),
                                             v_s.reshape(bs, ts, heads, HEAD_DIM), win_k, win_v)
    attn = put_sample(attn_p, attn_s[:, :ts].reshape(n_s, width))
    keep_p = min(W_MAX, t)
    win_k_p = jnp.transpose(proj3[heads:2 * heads, t - keep_p:t], (1, 0, 2))[None]
    win_v_p = jnp.transpose(proj3[2 * heads:3 * heads, t - keep_p:t], (1, 0, 2))[None]

    conv_p, u_last = _conv_prompt(proj3, t, 3 * heads, 3 * heads + cs, cs, conv_w, conv_b)
    conv_state_p = u_last[CONV_HALO - (CONV_W - 1):][None]
    conv_s, conv_state_s = _conv_sample(conv_state, _pad_rows(ga_s, SAMPLE_U_PAD), _pad_rows(gb_s, SAMPLE_U_PAD),
                                        conv_w, conv_b, ts)
    conv_all = put_sample(conv_p, conv_s[:, :ts].reshape(n_s, conv_ch))
    c_all = _ln_silu(conv_all, conv_ln_g, conv_ln_b)

    h1, h1_bf = _res_ln(x, _matmul([attn, c_all], w_out.astype(BF16)), ln1_g, ln1_b, alpha)

    qm = _matmul([h1_bf], w_mem_q.astype(BF16), out_dtype=BF16)
    mem_x = mem_prompt.reshape(bp * mem_tokens, d)
    mem_k_p = _matmul([mem_x], w_mem_k.astype(BF16))
    mem_v_p = _matmul([mem_x], w_mem_v.astype(BF16))
    om_s = _mem_attn(_pad_rows(qm[t:].reshape(bs, ts, d), SAMPLE_Q_PAD), mem_k_s.reshape(bs, mem_tokens, d),
                     mem_v_s.reshape(bs, mem_tokens, d), mem_heads)[:, :ts]
    om = put_sample(_mem_attn(qm[None], mem_k_p[None], mem_v_p[None], mem_heads)[0], om_s.reshape(n_s, d))
    h2, h2_packed, eid, gate = _ln_router(h1, _matmul([om], w_mem_o.astype(BF16)), ln2_g, ln2_b, alpha,
                                          w_rg, b_rg, w_re, b_re)
    row_tok, blk_cnt, plan, dest = _dispatch_plan(eid, n_experts)
    xs = _gather_rows(h2_packed, row_tok, blk_cnt)
    hb = _expert_up(xs, plan, w_gate, w_up)
    ys = _expert_down(hb, plan, w_down)
    y_p, y_s = _combine_ln(ys, dest[:, 0], dest[:, 1], h2, gate, ln3_g, ln3_b, alpha, t)

    mem_shape = (bp, mem_tokens, mem_heads, d // mem_heads)
    return (y_p.reshape(bp, t, d), y_s.reshape(bs, ts, d), win_k_p, win_v_p, conv_state_p,
            mem_k_p.reshape(mem_shape), mem_v_p.reshape(mem_shape), win_k_s, win_v_s, conv_state_s)


def kernel(x_prompt, x_sample, mem_prompt, cache_win_k, cache_win_v, state_conv, cache_mem_k, cache_mem_v, w_in, conv_w, conv_b, conv_ln_g, conv_ln_b, w_out, ln1_g, ln1_b, w_mem_q, w_mem_k, w_mem_v, w_mem_o, ln2_g, ln2_b, w_router_group, b_router_group, w_router_expert, b_router_expert, w_exp_gate, w_exp_up, w_exp_down, ln3_g, ln3_b):
    depth = w_in.shape[0]
    alpha = (2 * depth) ** 0.25
    hp, hs = x_prompt, x_sample
    per_layer = []
    for l in range(depth):
        outs = _layer(hp, hs, mem_prompt, cache_win_k[l], cache_win_v[l], state_conv[l], cache_mem_k[l], cache_mem_v[l],
                      w_in[l], conv_w[l], conv_b[l], conv_ln_g[l], conv_ln_b[l], w_out[l], ln1_g[l], ln1_b[l],
                      w_mem_q[l], w_mem_k[l], w_mem_v[l], w_mem_o[l], ln2_g[l], ln2_b[l],
                      w_router_group[l], b_router_group[l], w_router_expert[l], b_router_expert[l],
                      w_exp_gate[l], w_exp_up[l], w_exp_down[l], ln3_g[l], ln3_b[l], alpha)
        hp, hs = outs[0], outs[1]
        per_layer.append(outs[2:])
    stacked = [jnp.stack([layer[i] for layer in per_layer]) for i in range(8)]
    return (hp, hs, *stacked)
```
